```python
import math
import jax, jax.numpy as jnp
from jax import lax
import numpy as np

D_MODEL = 1024
BATCH = 4
SEQ = 4096
DEPTH = 2

S5_GROUP = 16
S5_GROUPS = D_MODEL // S5_GROUP
S5_STATE = 64
N_HEADS = 16
KV_HEADS = 4
HEAD_DIM = 64
CMP_BLOCK = 32
CMP_STRIDE = 16
CMP_HIDDEN = 2 * HEAD_DIM
SEL_BLOCK = 64
SEL_TOPN = 16
WINDOW = 512
Q_BLOCK = 128
ROPE_THETA = 500000.0
ROPE_DIM = HEAD_DIM // 4
D_FF = 2816
CONV_WIDTH = 3
EPS = 1e-6
NEG = -1e30
SEL_FORCE = 1e4
SEL_NEG = -1e4

kernel_name = "yoco_s5_nsa_convffn_trunk"


def rmsnorm(x, g):
    xf = x.astype(jnp.float32)
    y = xf * lax.rsqrt(jnp.mean(xf * xf, axis=-1, keepdims=True) + EPS)
    return (y * g.astype(jnp.float32)).astype(x.dtype)


def rope_partial(x, pos):
    half = ROPE_DIM // 2
    inv = ROPE_THETA ** (-jnp.arange(half, dtype=jnp.float32) / half)
    ang = pos.astype(jnp.float32)[:, None] * inv[None, :]
    cos = jnp.cos(ang)[:, None, :]
    sin = jnp.sin(ang)[:, None, :]
    xr = x[..., :ROPE_DIM].astype(jnp.float32)
    x1, x2 = xr[..., :half], xr[..., half:]
    rot = jnp.concatenate([x1 * cos - x2 * sin, x2 * cos + x1 * sin], axis=-1).astype(x.dtype)
    return jnp.concatenate([rot, x[..., ROPE_DIM:]], axis=-1)


def s5_mixer(u, lam_re, lam_im, log_dt, b_re, b_im, c_re, c_im, d_skip, w_glu):
    f32 = jnp.float32
    Bsz, T, _ = u.shape
    G, P, I = S5_GROUPS, S5_STATE, S5_GROUP
    uf = u.astype(f32).reshape(Bsz, T, G, I)
    dt = jnp.exp(log_dt.astype(f32))[:, None]
    lr, li = lam_re.astype(f32), lam_im.astype(f32)
    mag = jnp.exp(lr * dt)
    ab_re, ab_im = mag * jnp.cos(li * dt), mag * jnp.sin(li * dt)
    nr, ni = ab_re - 1.0, ab_im
    den = lr * lr + li * li
    coef_re = (nr * lr + ni * li) / den
    coef_im = (ni * lr - nr * li) / den
    br, bi = b_re.astype(f32), b_im.astype(f32)
    bb_re = coef_re[..., None] * br - coef_im[..., None] * bi
    bb_im = coef_re[..., None] * bi + coef_im[..., None] * br
    bu_re = jnp.einsum('btgi,gpi->btgp', uf, bb_re)
    bu_im = jnp.einsum('btgi,gpi->btgp', uf, bb_im)
    a_re = jnp.broadcast_to(ab_re[None, None], (1, T, G, P))
    a_im = jnp.broadcast_to(ab_im[None, None], (1, T, G, P))

    def combine(e1, e2):
        a1r, a1i, b1r, b1i = e1
        a2r, a2i, b2r, b2i = e2
        return (a2r * a1r - a2i * a1i, a2r * a1i + a2i * a1r,
                a2r * b1r - a2i * b1i + b2r, a2r * b1i + a2i * b1r + b2i)

    _, _, xr, xi = lax.associative_scan(combine, (a_re, a_im, bu_re, bu_im), axis=1)
    y = (jnp.einsum('btgp,gip->btgi', xr, c_re.astype(f32))
         - jnp.einsum('btgp,gip->btgi', xi, c_im.astype(f32)))
    y = y + uf * d_skip.astype(f32).reshape(G, I)
    y = jax.nn.gelu(y).reshape(Bsz, T, D_MODEL).astype(u.dtype)
    a, g = jnp.split(y @ w_glu, 2, axis=-1)
    return a * jax.nn.sigmoid(g)


def conv_ffn(h, w_in, conv_w, conv_b, w_out):
    gate, val = jnp.split(h @ w_in, 2, axis=-1)
    gate = lax.conv_general_dilated(
        gate, conv_w[:, None, :], window_strides=(1,), padding=[(CONV_WIDTH - 1, 0)],
        dimension_numbers=('NWC', 'WIO', 'NWC'), feature_group_count=D_FF) + conv_b
    return (jax.nn.gelu(gate) * val) @ w_out


def _compress(z, pe, w1, w2):
    Bsz, T, G, dh = z.shape
    r = CMP_BLOCK // CMP_STRIDE
    ch = z.reshape(Bsz, T // CMP_STRIDE, CMP_STRIDE, G, dh)
    n_cmp = T // CMP_STRIDE - r + 1
    blk = jnp.concatenate([ch[:, j:j + n_cmp] for j in range(r)], axis=2)
    blk = blk + pe[None, None, :, None, :]
    blk = blk.transpose(0, 1, 3, 2, 4).reshape(Bsz, n_cmp, G, CMP_BLOCK * dh)
    return jax.nn.gelu(blk @ w1) @ w2


def shared_kv(s, w_kv, pe_k, pe_v, k_w1, k_w2, v_w1, v_w2):
    Bsz, T, _ = s.shape
    kv = (s @ w_kv).reshape(Bsz, T, 6, KV_HEADS, HEAD_DIM)
    k_cmp_t, v_cmp_t = kv[:, :, 0], kv[:, :, 1]
    k_slc, v_slc = kv[:, :, 2], kv[:, :, 3]
    k_win, v_win = kv[:, :, 4], kv[:, :, 5]
    pos = jnp.arange(T)
    k_slc = rope_partial(k_slc, pos)
    k_win = rope_partial(k_win, pos)
    k_cmp = _compress(k_cmp_t, pe_k, k_w1, k_w2)
    v_cmp = _compress(v_cmp_t, pe_v, v_w1, v_w2)
    n_cmp = k_cmp.shape[1]
    k_cmp = rope_partial(k_cmp, jnp.arange(n_cmp) * CMP_STRIDE + CMP_BLOCK - 1)
    n_sb = T // SEL_BLOCK
    k_sel = k_slc.reshape(Bsz, n_sb, SEL_BLOCK, KV_HEADS, HEAD_DIM).transpose(0, 3, 1, 2, 4)
    v_sel = v_slc.reshape(Bsz, n_sb, SEL_BLOCK, KV_HEADS, HEAD_DIM).transpose(0, 3, 1, 2, 4)
    padw = ((0, 0), (WINDOW, 0), (0, 0), (0, 0))
    return (k_cmp, v_cmp, k_sel, v_sel, jnp.pad(k_win, padw), jnp.pad(v_win, padw))


def nsa_mixer(h, w_q, w_o, k_cmp, v_cmp, k_sel, v_sel, k_win_pad, v_win_pad):
    f32 = jnp.float32
    Bsz, T, _ = h.shape
    G, R, dh = KV_HEADS, N_HEADS // KV_HEADS, HEAD_DIM
    proj = h @ w_q
    q = proj[..., :N_HEADS * dh].reshape(Bsz, T, N_HEADS, dh)
    gates = jax.nn.sigmoid(proj[..., N_HEADS * dh:].astype(f32)).reshape(Bsz, T, N_HEADS, 3)
    q = rope_partial(q, jnp.arange(T)) * (dh ** -0.5)
    n_blk = T // Q_BLOCK
    q_blocks = q.reshape(Bsz, n_blk, Q_BLOCK, G, R, dh).transpose(1, 0, 2, 3, 4, 5)
    n_cmp = k_cmp.shape[1]
    n_sb = k_sel.shape[2]
    top_n = min(SEL_TOPN, n_sb)
    cmp_end = jnp.arange(n_cmp) * CMP_STRIDE + CMP_BLOCK - 1
    ci = np.arange(n_cmp)[:, None]
    sj = np.arange(n_sb)[None, :]
    overlap = jnp.asarray(((ci * CMP_STRIDE < (sj + 1) * SEL_BLOCK)
                           & (ci * CMP_STRIDE + CMP_BLOCK > sj * SEL_BLOCK)).astype(np.float32))
    b_ix = jnp.arange(Bsz)[:, None, None, None]
    g_ix = jnp.arange(G)[None, None, :, None]
    blk_id = jnp.arange(n_sb)

    def attend_block(args):
        qb, bi = args
        t = bi * Q_BLOCK + jnp.arange(Q_BLOCK)
        s1 = jnp.einsum('bqgrd,bngd->bqgrn', qb, k_cmp).astype(f32)
        m1 = (cmp_end[None, :] <= t[:, None])[None, :, None, None, :]
        p1 = jax.nn.softmax(jnp.where(m1, s1, NEG), axis=-1) * m1
        o_c = jnp.einsum('bqgrn,bngd->bqgrd', p1.astype(v_cmp.dtype), v_cmp)
        imp = jnp.einsum('bqgrn,nj->bqgj', p1, overlap)
        cur = t // SEL_BLOCK
        valid = (blk_id[None, :] <= cur[:, None])[None, :, None, :]
        forced = ((blk_id[None, :] == 0) | (blk_id[None, :] == cur[:, None])
                  | (blk_id[None, :] == cur[:, None] - 1))[None, :, None, :]
        score = jnp.where(forced, SEL_FORCE, jnp.where(valid, imp, SEL_NEG))
        vals, idx = lax.top_k(score, top_n)
        ks = k_sel[b_ix, g_ix, idx]
        vs = v_sel[b_ix, g_ix, idx]
        tok = idx[..., None] * SEL_BLOCK + jnp.arange(SEL_BLOCK)
        m2 = (tok <= t[None, :, None, None, None]) & (vals > 0.5 * SEL_NEG)[..., None]
        m2 = m2.reshape(Bsz, Q_BLOCK, G, 1, top_n * SEL_BLOCK)
        s2 = jnp.einsum('bqgrd,bqgnld->bqgrnl', qb, ks).astype(f32).reshape(Bsz, Q_BLOCK, G, R, top_n * SEL_BLOCK)
        p2 = jax.nn.softmax(jnp.where(m2, s2, NEG), axis=-1)
        o_s = jnp.einsum('bqgrk,bqgkd->bqgrd', p2.astype(vs.dtype), vs.reshape(Bsz, Q_BLOCK, G, top_n * SEL_BLOCK, dh))
        start = bi * Q_BLOCK
        kw = lax.dynamic_slice_in_dim(k_win_pad, start, Q_BLOCK + WINDOW, axis=1)
        vw = lax.dynamic_slice_in_dim(v_win_pad, start, Q_BLOCK + WINDOW, axis=1)
        kpos = start - WINDOW + jnp.arange(Q_BLOCK + WINDOW)
        m3 = ((kpos[None, :] >= 0) & (kpos[None, :] <= t[:, None])
              & (t[:, None] - kpos[None, :] < WINDOW))[None, :, None, None, :]
        s3 = jnp.einsum('bqgrd,bkgd->bqgrk', qb, kw).astype(f32)
        p3 = jax.nn.softmax(jnp.where(m3, s3, NEG), axis=-1)
        o_w = jnp.einsum('bqgrk,bkgd->bqgrd', p3.astype(vw.dtype), vw)
        return (o_c, o_s, o_w)

    o_c, o_s, o_w = lax.map(attend_block, (q_blocks, jnp.arange(n_blk)))

    def unblock(o):
        return o.transpose(1, 0, 2, 3, 4, 5).reshape(Bsz, T, N_HEADS, dh).astype(f32)

    o = (gates[..., 0:1] * unblock(o_c) + gates[..., 1:2] * unblock(o_s)
         + gates[..., 2:3] * unblock(o_w)).astype(h.dtype)
    return o.reshape(Bsz, T, N_HEADS * dh) @ w_o


def setup_inputs(seed: int = 0) -> dict:
    key = jax.random.key(seed)
    ks = jax.random.split(key, 32)
    n_a = DEPTH // 2
    n_b = DEPTH - n_a
    D, G, P, I = D_MODEL, S5_GROUPS, S5_STATE, S5_GROUP
    nrm = jax.random.normal
    f32 = jnp.float32
    x = nrm(ks[0], (BATCH, SEQ, D), f32)
    a_lam_re = -0.5 * jnp.exp(0.05 * nrm(ks[1], (n_a, G, P), f32))
    a_lam_im = math.pi * jnp.arange(P, dtype=f32)[None, None, :] + 0.05 * nrm(ks[2], (n_a, G, P), f32)
    a_log_dt = jax.random.uniform(ks[3], (n_a, G), f32, math.log(1e-3), math.log(1e-1))
    a_b_re = nrm(ks[4], (n_a, G, P, I), f32) * (2 * I) ** -0.5
    a_b_im = nrm(ks[5], (n_a, G, P, I), f32) * (2 * I) ** -0.5
    a_c_re = nrm(ks[6], (n_a, G, I, P), f32) * P ** -0.5
    a_c_im = nrm(ks[7], (n_a, G, I, P), f32) * P ** -0.5
    a_d = nrm(ks[8], (n_a, D), f32)
    a_w_glu = nrm(ks[9], (n_a, D, 2 * D), f32) * D ** -0.5
    b_w_q = nrm(ks[10], (n_b, D, N_HEADS * HEAD_DIM + 3 * N_HEADS), f32) * D ** -0.5
    b_w_o = nrm(ks[11], (n_b, N_HEADS * HEAD_DIM, D), f32) * (N_HEADS * HEAD_DIM) ** -0.5
    kv_norm_g = 1.0 + 0.05 * nrm(ks[12], (D,), f32)
    w_kv = nrm(ks[13], (D, 6 * KV_HEADS * HEAD_DIM), f32) * D ** -0.5
    cmp_pe_k = 0.1 * nrm(ks[14], (CMP_BLOCK, HEAD_DIM), f32)
    cmp_pe_v = 0.1 * nrm(ks[15], (CMP_BLOCK, HEAD_DIM), f32)
    cmp_k_w1 = nrm(ks[16], (CMP_BLOCK * HEAD_DIM, CMP_HIDDEN), f32) * (CMP_BLOCK * HEAD_DIM) ** -0.5
    cmp_k_w2 = nrm(ks[17], (CMP_HIDDEN, HEAD_DIM), f32) * CMP_HIDDEN ** -0.5
    cmp_v_w1 = nrm(ks[18], (CMP_BLOCK * HEAD_DIM, CMP_HIDDEN), f32) * (CMP_BLOCK * HEAD_DIM) ** -0.5
    cmp_v_w2 = nrm(ks[19], (CMP_HIDDEN, HEAD_DIM), f32) * CMP_HIDDEN ** -0.5
    mix_pre_g = 1.0 + 0.05 * nrm(ks[20], (DEPTH, D), f32)
    mix_post_g = 1.0 + 0.05 * nrm(ks[21], (DEPTH, D), f32)
    ffn_pre_g = 1.0 + 0.05 * nrm(ks[22], (DEPTH, D), f32)
    ffn_post_g = 1.0 + 0.05 * nrm(ks[23], (DEPTH, D), f32)
    ffn_w_in = nrm(ks[24], (DEPTH, D, 2 * D_FF), f32) * D ** -0.5
    ffn_conv_w = nrm(ks[25], (DEPTH, CONV_WIDTH, D_FF), f32) * CONV_WIDTH ** -0.5
    ffn_conv_b = 0.01 * nrm(ks[26], (DEPTH, D_FF), f32)
    ffn_w_out = nrm(ks[27], (DEPTH, D_FF, D), f32) * D_FF ** -0.5
    return {"x": x, "a_lam_re": a_lam_re, "a_lam_im": a_lam_im, "a_log_dt": a_log_dt,
            "a_b_re": a_b_re, "a_b_im": a_b_im, "a_c_re": a_c_re, "a_c_im": a_c_im,
            "a_d": a_d, "a_w_glu": a_w_glu, "b_w_q": b_w_q, "b_w_o": b_w_o,
            "kv_norm_g": kv_norm_g, "w_kv": w_kv, "cmp_pe_k": cmp_pe_k, "cmp_pe_v": cmp_pe_v,
            "cmp_k_w1": cmp_k_w1, "cmp_k_w2": cmp_k_w2, "cmp_v_w1": cmp_v_w1, "cmp_v_w2": cmp_v_w2,
            "mix_pre_g": mix_pre_g, "mix_post_g": mix_post_g, "ffn_pre_g": ffn_pre_g,
            "ffn_post_g": ffn_post_g, "ffn_w_in": ffn_w_in, "ffn_conv_w": ffn_conv_w,
            "ffn_conv_b": ffn_conv_b, "ffn_w_out": ffn_w_out}


def reference(x, a_lam_re, a_lam_im, a_log_dt, a_b_re, a_b_im, a_c_re, a_c_im, a_d, a_w_glu,
              b_w_q, b_w_o, kv_norm_g, w_kv, cmp_pe_k, cmp_pe_v, cmp_k_w1, cmp_k_w2,
              cmp_v_w1, cmp_v_w2, mix_pre_g, mix_post_g, ffn_pre_g, ffn_post_g,
              ffn_w_in, ffn_conv_w, ffn_conv_b, ffn_w_out):
    n_a = DEPTH // 2
    kv = None
    for layer in range(DEPTH):
        h = rmsnorm(x, mix_pre_g[layer])
        if layer < n_a:
            i = layer
            m = s5_mixer(h, a_lam_re[i], a_lam_im[i], a_log_dt[i], a_b_re[i], a_b_im[i],
                         a_c_re[i], a_c_im[i], a_d[i], a_w_glu[i])
        else:
            j = layer - n_a
            m = nsa_mixer(h, b_w_q[j], b_w_o[j], *kv)
        x = x + rmsnorm(m, mix_post_g[layer])
        h = rmsnorm(x, ffn_pre_g[layer])
        f = conv_ffn(h, ffn_w_in[layer], ffn_conv_w[layer], ffn_conv_b[layer], ffn_w_out[layer])
        x = x + rmsnorm(f, ffn_post_g[layer])
        if layer == n_a - 1:
            kv = shared_kv(rmsnorm(x, kv_norm_g), w_kv, cmp_pe_k, cmp_pe_v,
                           cmp_k_w1, cmp_k_w2, cmp_v_w1, cmp_v_w2)
    return x
```

```python
import functools
import math

import numpy as np
import jax
import jax.numpy as jnp
from jax import lax
from jax.experimental import pallas as pl
from jax.experimental.pallas import tpu as pltpu

F32 = jnp.float32
BF16 = jnp.bfloat16

S5_GROUP = 16
S5_STATE = 64
N_HEADS = 16
KV_HEADS = 4
HEAD_DIM = 64
CMP_BLOCK = 32
CMP_STRIDE = 16
SEL_BLOCK = 64
SEL_TOPN = 16
WINDOW = 512
Q_BLOCK = 128
ROPE_THETA = 500000.0
ROPE_DIM = HEAD_DIM // 4
CONV_WIDTH = 3
EPS = 1e-6
NEG = -1e30
SEL_FORCE = 1e4
SEL_NEG = -1e4

LANES = 128
S5_CHUNK = 16
S5_GROUPS_PER_STEP = 8
CONV_HALO = 16
VMEM_LIMIT = 48 * 1024 * 1024


def _cparams(n_axes):
    return pltpu.CompilerParams(dimension_semantics=("arbitrary",) * n_axes,
                                vmem_limit_bytes=VMEM_LIMIT)


def _rms(x, g):
    return x * lax.rsqrt(jnp.mean(x * x, axis=-1, keepdims=True) + EPS) * g


def _gelu(x):
    return jax.nn.gelu(x, approximate=True)


def _dot(a, b):
    return jnp.dot(a, b, preferred_element_type=F32)


def _dot_nt(a, b):
    return lax.dot_general(a, b, (((1,), (1,)), ((), ())), preferred_element_type=F32)


def _row_tile(n, want):
    t = min(n, want)
    assert n % t == 0
    return t


def _s5_tables(lam_re, lam_im, log_dt, b_re, b_im, c_re, c_im, n_chunks):
    hp = lax.Precision.HIGHEST
    L = S5_CHUNK
    G, P = lam_re.shape
    I = b_re.shape[-1]
    dt = jnp.exp(log_dt.astype(F32))[:, None]
    lr, li = lam_re.astype(F32), lam_im.astype(F32)
    mag = jnp.exp(lr * dt)
    ab_re, ab_im = mag * jnp.cos(li * dt), mag * jnp.sin(li * dt)
    nr, ni = ab_re - 1.0, ab_im
    den = lr * lr + li * li
    coef_re = (nr * lr + ni * li) / den
    coef_im = (ni * lr - nr * li) / den
    br, bi = b_re.astype(F32), b_im.astype(F32)
    bb_re = coef_re[..., None] * br - coef_im[..., None] * bi
    bb_im = coef_re[..., None] * bi + coef_im[..., None] * br
    pr = [jnp.ones_like(ab_re)]
    pi = [jnp.zeros_like(ab_re)]
    for _ in range(L):
        r, i = pr[-1], pi[-1]
        pr.append(r * ab_re - i * ab_im)
        pi.append(r * ab_im + i * ab_re)
    pw_re = jnp.stack(pr)
    pw_im = jnp.stack(pi)
    cr, ci = c_re.astype(F32), c_im.astype(F32)
    cl_re = cr[None] * pw_re[:, :, None, :] - ci[None] * pw_im[:, :, None, :]
    cl_im = cr[None] * pw_im[:, :, None, :] + ci[None] * pw_re[:, :, None, :]
    kk = (jnp.einsum('kgop,gpi->gkio', cl_re[:L], bb_re, precision=hp)
          - jnp.einsum('kgop,gpi->gkio', cl_im[:L], bb_im, precision=hp))
    s_ix = np.arange(L)[:, None]
    r_ix = np.arange(L)[None, :]
    lag = np.maximum(r_ix - s_ix, 0)
    toep = kk[:, lag]
    toep = jnp.where(jnp.asarray(r_ix >= s_ix)[None, :, :, None, None], toep, 0.0)
    toep = toep.transpose(0, 1, 3, 2, 4).reshape(G, L * I, L * I)
    rev_re = pw_re[L - 1 - np.arange(L)]
    rev_im = pw_im[L - 1 - np.arange(L)]
    pb_re = rev_re[..., None] * bb_re[None] - rev_im[..., None] * bb_im[None]
    pb_im = rev_re[..., None] * bb_im[None] + rev_im[..., None] * bb_re[None]
    pmat = jnp.concatenate([pb_re.transpose(1, 0, 3, 2).reshape(G, L * I, P),
                            pb_im.transpose(1, 0, 3, 2).reshape(G, L * I, P)], axis=-1)
    w1 = jnp.concatenate([toep, pmat], axis=-1).astype(BF16)
    q_re = cl_re[1:].transpose(1, 3, 0, 2).reshape(G, P, L * I)
    q_im = cl_im[1:].transpose(1, 3, 0, 2).reshape(G, P, L * I)
    qmat = jnp.concatenate([q_re, -q_im], axis=1).astype(BF16)
    n_steps = max(1, int(math.ceil(math.log2(n_chunks)))) if n_chunks > 1 else 0
    mr, mi = pw_re[L], pw_im[L]
    a1, a2 = [], []
    for _ in range(max(n_steps, 1)):
        a1.append(jnp.concatenate([mr, mr], axis=-1))
        a2.append(jnp.concatenate([-mi, mi], axis=-1))
        mr, mi = mr * mr - mi * mi, 2.0 * mr * mi
    pad = (-len(a1)) % 8
    a1 = jnp.stack(a1 + [jnp.zeros_like(a1[0])] * pad, axis=1)
    a2 = jnp.stack(a2 + [jnp.zeros_like(a2[0])] * pad, axis=1)
    return w1, qmat, a1, a2, n_steps


def _norm_kernel(x_ref, g_ref, o_ref):
    o_ref[...] = _rms(x_ref[...], g_ref[...]).astype(o_ref.dtype)


def _norm_call(x2d, g, out_dtype, tm):
    n, d = x2d.shape
    return pl.pallas_call(
        _norm_kernel,
        grid=(n // tm,),
        in_specs=[pl.BlockSpec((tm, d), lambda i: (i, 0)),
                  pl.BlockSpec((1, d), lambda i: (0, 0))],
        out_specs=pl.BlockSpec((tm, d), lambda i: (i, 0)),
        out_shape=jax.ShapeDtypeStruct((n, d), out_dtype),
        compiler_params=_cparams(1),
        name="s5_prenorm",
    )(x2d, g.reshape(1, d))


def _s5_scan_kernel(u_ref, w1_ref, q_ref, a1_ref, a2_ref, y_ref, *, n_steps):
    gt, n_chunks = u_ref.shape[1], u_ref.shape[2]
    wid = S5_CHUNK * S5_GROUP
    row = lax.broadcasted_iota(jnp.int32, (n_chunks, 2 * S5_STATE), 0)
    for g in range(gt):
        r = _dot(u_ref[0, g], w1_ref[g])
        y_intra = r[:, :wid]
        x = r[:, wid:]
        for j in range(n_steps):
            k = 1 << j
            sh = jnp.where(row >= k, pltpu.roll(x, k, 0), 0.0)
            x = x + a1_ref[g, j:j + 1, :] * sh + a2_ref[g, j:j + 1, :] * pltpu.roll(sh, S5_STATE, 1)
        xp = jnp.where(row >= 1, pltpu.roll(x, 1, 0), 0.0)
        y_ref[0, g] = y_intra + _dot(xp.astype(BF16), q_ref[g])


def _s5_scan_call(uc, w1, qmat, a1, a2, n_steps):
    b, g, c, wid = uc.shape
    gt = min(S5_GROUPS_PER_STEP, g)
    return pl.pallas_call(
        functools.partial(_s5_scan_kernel, n_steps=n_steps),
        grid=(b, g // gt),
        in_specs=[pl.BlockSpec((1, gt, c, wid), lambda i, j: (i, j, 0, 0)),
                  pl.BlockSpec((gt,) + w1.shape[1:], lambda i, j: (j, 0, 0)),
                  pl.BlockSpec((gt,) + qmat.shape[1:], lambda i, j: (j, 0, 0)),
                  pl.BlockSpec((gt,) + a1.shape[1:], lambda i, j: (j, 0, 0)),
                  pl.BlockSpec((gt,) + a2.shape[1:], lambda i, j: (j, 0, 0))],
        out_specs=pl.BlockSpec((1, gt, c, wid), lambda i, j: (i, j, 0, 0)),
        out_shape=jax.ShapeDtypeStruct((b, g, c, wid), F32),
        compiler_params=_cparams(2),
        name="s5_scan",
    )(uc, w1, qmat, a1, a2)


def _s5_out_kernel(x_ref, y_ref, gpre_ref, d_ref, w_ref, gpost_ref, o_ref):
    x = x_ref[...]
    u = _rms(x, gpre_ref[...])
    z = _gelu(y_ref[...] + u * d_ref[...]).astype(BF16)
    ag = _dot(z, w_ref[...])
    d = x.shape[-1]
    m = ag[:, :d] * jax.nn.sigmoid(ag[:, d:])
    o_ref[...] = x + _rms(m, gpost_ref[...])


def _s5_out_call(x2d, y2d, gpre, dskip, wglu, gpost, tm):
    n, d = x2d.shape
    row = lambda i: (i, 0)
    fix = lambda i: (0, 0)
    return pl.pallas_call(
        _s5_out_kernel,
        grid=(n // tm,),
        in_specs=[pl.BlockSpec((tm, d), row), pl.BlockSpec((tm, d), row),
                  pl.BlockSpec((1, d), fix), pl.BlockSpec((1, d), fix),
                  pl.BlockSpec(wglu.shape, fix), pl.BlockSpec((1, d), fix)],
        out_specs=pl.BlockSpec((tm, d), row),
        out_shape=jax.ShapeDtypeStruct((n, d), F32),
        compiler_params=_cparams(1),
        name="s5_glu_out",
    )(x2d, y2d, gpre.reshape(1, d), dskip.reshape(1, d), wglu, gpost.reshape(1, d))


def _s5_layer(x, gpre, gpost, lam_re, lam_im, log_dt, b_re, b_im, c_re, c_im, d_skip, w_glu):
    b, t, d = x.shape
    g = d // S5_GROUP
    L = S5_CHUNK
    c = t // L
    n = b * t
    tm = _row_tile(n, 512)
    w1, qmat, a1, a2, n_steps = _s5_tables(lam_re, lam_im, log_dt, b_re, b_im, c_re, c_im, c)
    x2d = x.reshape(n, d)
    u = _norm_call(x2d, gpre, BF16, tm)
    uc = u.reshape(b, c, L, g, S5_GROUP).transpose(0, 3, 1, 2, 4).reshape(b, g, c, L * S5_GROUP)
    yc = _s5_scan_call(uc, w1, qmat, a1, a2, n_steps)
    y2d = yc.reshape(b, g, c, L, S5_GROUP).transpose(0, 2, 3, 1, 4).reshape(n, d)
    out = _s5_out_call(x2d, y2d, gpre, d_skip, w_glu.astype(BF16), gpost, tm)
    return out.reshape(b, t, d)


def _ffn_kernel(xprev_ref, x_ref, gpre_ref, wg_ref, wv_ref, cw_ref, cb_ref, wo_ref, gpost_ref,
                o_ref, xn_ref, acc_ref, *, seq_tiles):
    i = pl.program_id(0)
    j = pl.program_id(1)
    h = CONV_HALO

    @pl.when(j == 0)
    def _():
        g = gpre_ref[...]
        xn_ref[h:, :] = _rms(x_ref[...], g).astype(BF16)
        keep = jnp.where(i % seq_tiles == 0, 0.0, 1.0)
        xn_ref[:h, :] = (_rms(xprev_ref[...], g) * keep).astype(BF16)
        acc_ref[...] = jnp.zeros_like(acc_ref)

    xn = xn_ref[...]
    gate = _dot(xn, wg_ref[...])
    val = _dot(xn[h:], wv_ref[...])
    cw = cw_ref[...]
    conv = (cw[0:1] * pltpu.roll(gate, 2, 0) + cw[1:2] * pltpu.roll(gate, 1, 0)
            + cw[2:3] * gate + cb_ref[...])
    act = _gelu(conv[h:]) * val
    acc_ref[...] += _dot(act.astype(BF16), wo_ref[...])

    @pl.when(j == pl.num_programs(1) - 1)
    def _():
        o_ref[...] = x_ref[...] + _rms(acc_ref[...], gpost_ref[...])


def _ffn_layer(x, gpre, gpost, w_in, conv_w, conv_b, w_out):
    b, t, d = x.shape
    f = w_out.shape[0]
    n = b * t
    tm = _row_tile(t, 1024)
    tf = 256 if f % 256 == 0 else LANES
    assert f % tf == 0 and tm % CONV_HALO == 0
    nf = f // tf
    h = CONV_HALO
    x2d = x.reshape(n, d)
    cw = jnp.zeros((8, f), F32).at[:CONV_WIDTH].set(conv_w.astype(F32))
    w_in16 = w_in.astype(BF16)
    out = pl.pallas_call(
        functools.partial(_ffn_kernel, seq_tiles=t // tm),
        grid=(n // tm, nf),
        in_specs=[pl.BlockSpec((h, d), lambda i, j: (jnp.maximum(i * (tm // h) - 1, 0), 0)),
                  pl.BlockSpec((tm, d), lambda i, j: (i, 0)),
                  pl.BlockSpec((1, d), lambda i, j: (0, 0)),
                  pl.BlockSpec((d, tf), lambda i, j: (0, j)),
                  pl.BlockSpec((d, tf), lambda i, j: (0, nf + j)),
                  pl.BlockSpec((8, tf), lambda i, j: (0, j)),
                  pl.BlockSpec((1, tf), lambda i, j: (0, j)),
                  pl.BlockSpec((tf, d), lambda i, j: (j, 0)),
                  pl.BlockSpec((1, d), lambda i, j: (0, 0))],
        out_specs=pl.BlockSpec((tm, d), lambda i, j: (i, 0)),
        out_shape=jax.ShapeDtypeStruct((n, d), F32),
        scratch_shapes=[pltpu.VMEM((tm + h, d), BF16), pltpu.VMEM((tm, d), F32)],
        compiler_params=_cparams(2),
        name="conv_ffn",
    )(x2d, x2d, gpre.reshape(1, d), w_in16, w_in16, cw, conv_b.reshape(1, f).astype(F32),
      w_out.astype(BF16), gpost.reshape(1, d))
    return out.reshape(b, t, d)


def _rope_tables(pos, width):
    half = ROPE_DIM // 2
    inv = ROPE_THETA ** (-jnp.arange(half, dtype=F32) / half)
    ang = pos.astype(F32)[:, None] * inv[None, :]
    cos, sin = jnp.cos(ang), jnp.sin(ang)
    rest = HEAD_DIM - ROPE_DIM
    n = pos.shape[0]
    c = jnp.concatenate([cos, cos, jnp.ones((n, rest), F32)], axis=-1)
    s = jnp.concatenate([-sin, sin, jnp.zeros((n, rest), F32)], axis=-1)
    reps = width // HEAD_DIM
    return jnp.tile(c, (1, reps)), jnp.tile(s, (1, reps))


def _swap_perm(width):
    half = ROPE_DIM // 2
    d = np.arange(width)
    dd = d % HEAD_DIM
    return np.where(dd < half, d + half, np.where(dd < ROPE_DIM, d - half, d))


def _kv_kernel(x_ref, g_ref, w_ref, cos_ref, sin_ref, k_ref, v_ref, c_ref):
    wk = k_ref.shape[-1]
    sn = _rms(x_ref[...], g_ref[...]).astype(BF16)
    r = _dot(sn, w_ref[...])
    reps = wk // LANES
    cos = jnp.concatenate([cos_ref[...]] * reps, axis=1)
    sin = jnp.concatenate([sin_ref[...]] * reps, axis=1)
    k_ref[...] = (r[:, :wk] * cos + r[:, wk:2 * wk] * sin).astype(k_ref.dtype)
    v_ref[...] = r[:, 2 * wk:3 * wk].astype(v_ref.dtype)
    c_ref[...] = r[:, 3 * wk:]


def _kv_call(x2d, g, w_all, cos, sin, t, tm):
    n, d = x2d.shape
    wk = KV_HEADS * 2 * HEAD_DIM
    tpt = t // tm
    row = lambda i: (i, 0)
    fix = lambda i: (0, 0)
    tab = lambda i: (i % tpt, 0)
    return pl.pallas_call(
        _kv_kernel,
        grid=(n // tm,),
        in_specs=[pl.BlockSpec((tm, d), row), pl.BlockSpec((1, d), fix),
                  pl.BlockSpec(w_all.shape, fix),
                  pl.BlockSpec((tm, LANES), tab), pl.BlockSpec((tm, LANES), tab)],
        out_specs=[pl.BlockSpec((tm, wk), row)] * 3,
        out_shape=[jax.ShapeDtypeStruct((n, wk), BF16), jax.ShapeDtypeStruct((n, wk), BF16),
                   jax.ShapeDtypeStruct((n, wk), F32)],
        compiler_params=_cparams(1),
        name="nsa_kv_proj",
    )(x2d, g.reshape(1, d), w_all, cos, sin)


def _cmp_kernel(z_ref, pe_ref, w1_ref, w2_ref, cos_ref, sin_ref, o_ref, *, n_cmp):
    c = z_ref.shape[3]
    half = z_ref.shape[-1]
    res = []
    for kv in range(2):
        z = z_ref[0, 0, kv]
        top = _dot((z + pe_ref[kv, 0:1, :]).astype(BF16), w1_ref[kv, :half, :])
        bot = _dot((z + pe_ref[kv, 1:2, :]).astype(BF16), w1_ref[kv, half:, :])
        hdn = _gelu(top + pltpu.roll(bot, c - 1, 0)).astype(BF16)
        res.append(hdn)
    k_main = _dot(res[0], w2_ref[0])
    k_swap = _dot(res[0], w2_ref[1])
    v_main = _dot(res[1], w2_ref[2])
    out = k_main * cos_ref[...] + k_swap * sin_ref[...] + v_main
    row = lax.broadcasted_iota(jnp.int32, out.shape, 0)
    o_ref[0, 0] = jnp.where(row < n_cmp, out, 0.0).astype(o_ref.dtype)


def _cmp_call(zc, pe, w1, w2, cos, sin, n_cmp):
    b, g, _, c, wid = zc.shape
    return pl.pallas_call(
        functools.partial(_cmp_kernel, n_cmp=n_cmp),
        grid=(b, g),
        in_specs=[pl.BlockSpec((1, 1, 2, c, wid), lambda i, j: (i, j, 0, 0, 0)),
                  pl.BlockSpec(pe.shape, lambda i, j: (0, 0, 0)),
                  pl.BlockSpec(w1.shape, lambda i, j: (0, 0, 0)),
                  pl.BlockSpec(w2.shape, lambda i, j: (0, 0, 0)),
                  pl.BlockSpec(cos.shape, lambda i, j: (0, 0)),
                  pl.BlockSpec(sin.shape, lambda i, j: (0, 0))],
        out_specs=pl.BlockSpec((1, 1, c, LANES), lambda i, j: (i, j, 0, 0)),
        out_shape=jax.ShapeDtypeStruct((b, g, c, LANES), BF16),
        compiler_params=_cparams(2),
        name="nsa_compress",
    )(zc, pe, w1, w2, cos, sin)


def _q_kernel(x_ref, g_ref, w_ref, cos_ref, sin_ref, q_ref, gate_ref):
    d = q_ref.shape[-1]
    hn = _rms(x_ref[...], g_ref[...]).astype(BF16)
    r = _dot(hn, w_ref[...])
    reps = d // LANES
    cos = jnp.concatenate([cos_ref[...]] * reps, axis=1)
    sin = jnp.concatenate([sin_ref[...]] * reps, axis=1)
    q_ref[...] = ((r[:, :d] * cos + r[:, d:2 * d] * sin) * (HEAD_DIM ** -0.5)).astype(q_ref.dtype)
    gate_ref[...] = jax.nn.sigmoid(r[:, 2 * d:])


def _q_call(x2d, g, w_all, cos, sin, t, tm):
    n, d = x2d.shape
    dq = N_HEADS * HEAD_DIM
    tpt = t // tm
    row = lambda i: (i, 0)
    fix = lambda i: (0, 0)
    tab = lambda i: (i % tpt, 0)
    return pl.pallas_call(
        _q_kernel,
        grid=(n // tm,),
        in_specs=[pl.BlockSpec((tm, d), row), pl.BlockSpec((1, d), fix),
                  pl.BlockSpec(w_all.shape, fix),
                  pl.BlockSpec((tm, LANES), tab), pl.BlockSpec((tm, LANES), tab)],
        out_specs=[pl.BlockSpec((tm, dq), row), pl.BlockSpec((tm, LANES), row)],
        out_shape=[jax.ShapeDtypeStruct((n, dq), BF16), jax.ShapeDtypeStruct((n, LANES), F32)],
        compiler_params=_cparams(1),
        name="nsa_q_proj",
    )(x2d, g.reshape(1, d), w_all, cos, sin)


def _attn_kernel(q_ref, kvc_ref, k2_ref, v2_ref, ov_ref, esel_ref, oc_ref, os_ref, ow_ref,
                 selexp_ref, *, n_sb, top_n):
    qb = q_ref.shape[1]
    rep = N_HEADS // KV_HEADS
    hd = HEAD_DIM
    i = pl.program_id(2)
    start = i * qb
    lane = lax.broadcasted_iota(jnp.int32, (qb, LANES), 1)
    lo = lane < hd

    q = q_ref[0].astype(F32)
    zero = jnp.zeros((qb, LANES), F32)
    q_lo, q_hi = [], []
    for p in range(rep * hd // LANES):
        part = q[:, p * LANES:(p + 1) * LANES]
        swapped = pltpu.roll(part, hd, 1)
        q_lo += [jnp.where(lo, part, zero), jnp.where(lo, swapped, zero)]
        q_hi += [jnp.where(lo, zero, swapped), jnp.where(lo, zero, part)]
    q_lo = jnp.concatenate(q_lo, axis=0).astype(BF16)
    q_hi = jnp.concatenate(q_hi, axis=0).astype(BF16)
    rows = rep * qb
    t_rows = start + (lax.broadcasted_iota(jnp.int32, (rows, LANES), 0) & (qb - 1))
    lane_r = lax.broadcasted_iota(jnp.int32, (rows, LANES), 1)

    def unstack(acc, valid_low):
        outs = []
        for p in range(rep // 2):
            a0 = acc[(2 * p) * qb:(2 * p + 1) * qb]
            a1 = acc[(2 * p + 1) * qb:(2 * p + 2) * qb]
            if valid_low:
                outs.append(jnp.where(lo, a0, pltpu.roll(a1, hd, 1)))
            else:
                outs.append(jnp.where(lo, pltpu.roll(a0, hd, 1), a1))
        return jnp.concatenate(outs, axis=1)

    kvc = kvc_ref[0, 0]
    n_c = kvc.shape[0]
    s1 = _dot_nt(q_lo, kvc)
    n_idx = lax.broadcasted_iota(jnp.int32, (rows, n_c), 1)
    t1 = start + (lax.broadcasted_iota(jnp.int32, (rows, n_c), 0) & (qb - 1))
    m1 = n_idx * CMP_STRIDE + (CMP_BLOCK - 1) <= t1
    s1m = jnp.where(m1, s1, NEG)
    e1 = jnp.where(m1, jnp.exp(s1m - jnp.max(s1m, axis=1, keepdims=True)), 0.0)
    l1 = jnp.sum(e1, axis=1, keepdims=True)
    p1 = e1 / jnp.where(l1 > 0.0, l1, 1.0)
    oc_ref[0] = unstack(_dot(p1.astype(BF16), kvc), False)

    psum = p1[0:qb]
    for r in range(1, rep):
        psum = psum + p1[r * qb:(r + 1) * qb]
    imp = jnp.dot(psum, ov_ref[...], preferred_element_type=F32, precision=lax.Precision.HIGHEST)
    t_q = start + lax.broadcasted_iota(jnp.int32, (qb, LANES), 0)
    cur = jnp.right_shift(t_q, int(math.log2(SEL_BLOCK)))
    valid = lane <= cur
    forced = (lane == 0) | (lane == cur) | (lane == cur - 1)
    score = jnp.where(forced, SEL_FORCE, jnp.where(valid, imp, SEL_NEG))
    score = jnp.where(lane < n_sb, score, -jnp.inf)
    lane_f = lane.astype(F32)
    work = score
    sel = jnp.zeros((qb, LANES), F32)
    for _ in range(top_n):
        mx = jnp.max(work, axis=1, keepdims=True)
        first = jnp.min(jnp.where(work == mx, lane_f, float(LANES)), axis=1, keepdims=True)
        hit = lane_f == first
        sel = jnp.where(hit, 1.0, sel)
        work = jnp.where(hit, -jnp.inf, work)
    sel = jnp.where(score > 0.5 * SEL_NEG, sel, 0.0)
    t_len = selexp_ref.shape[1]
    key = lax.broadcasted_iota(jnp.int32, (qb, t_len), 1)
    t_k = start + lax.broadcasted_iota(jnp.int32, (qb, t_len), 0)
    selexp_ref[...] = jnp.where(key <= t_k, _dot(sel.astype(BF16), esel_ref[...]), 0.0)

    def flash_step(qmat, kt, mask, carry):
        m, l, acc = carry
        off = pl.multiple_of(kt * LANES, LANES)
        s = _dot_nt(qmat, k2_ref[0, pl.ds(off, LANES), :])
        sm = jnp.where(mask, s, NEG)
        m_new = jnp.maximum(m, jnp.max(sm, axis=1, keepdims=True))
        alpha = jnp.exp(m - m_new)
        p = jnp.where(mask, jnp.exp(sm - m_new), 0.0)
        l = alpha * l + jnp.sum(p, axis=1, keepdims=True)
        acc = alpha * acc + _dot(p.astype(BF16), v2_ref[0, pl.ds(off, LANES), :])
        return m_new, l, acc

    init = (jnp.full((rows, 1), NEG, F32), jnp.zeros((rows, 1), F32), jnp.zeros((rows, LANES), F32))

    def sel_body(kt, carry):
        off = pl.multiple_of(kt * LANES, LANES)
        mk = selexp_ref[:, pl.ds(off, LANES)]
        mask = jnp.concatenate([mk] * rep, axis=0) > 0.5
        return flash_step(q_lo, kt, mask, carry)

    _, l2, acc2 = lax.fori_loop(0, i + 1, sel_body, init)
    os_ref[0] = unstack(acc2 / l2, True)

    def win_body(kt, carry):
        kpos = kt * LANES + lane_r
        mask = (kpos <= t_rows) & (t_rows - kpos < WINDOW)
        return flash_step(q_hi, kt, mask, carry)

    n_win = WINDOW // LANES
    _, l3, acc3 = lax.fori_loop(jnp.maximum(i - n_win, 0), i + 1, win_body, init)
    ow_ref[0] = unstack(acc3 / l3, False)


def _attn_call(q, kvc, k2, v2, overlap, esel, n_sb):
    b, t, dq = q.shape
    g = KV_HEADS
    gw = dq // g
    c = kvc.shape[2]
    qb = Q_BLOCK
    assert qb == LANES and t % qb == 0
    blk_q = pl.BlockSpec((1, qb, gw), lambda bi, gi, i: (bi, i, gi))
    blk_kv = pl.BlockSpec((1, t, LANES), lambda bi, gi, i: (bi, 0, gi))
    out = jax.ShapeDtypeStruct((b, t, dq), F32)
    return pl.pallas_call(
        functools.partial(_attn_kernel, n_sb=n_sb, top_n=min(SEL_TOPN, n_sb)),
        grid=(b, g, t // qb),
        in_specs=[blk_q,
                  pl.BlockSpec((1, 1, c, LANES), lambda bi, gi, i: (bi, gi, 0, 0)),
                  blk_kv, blk_kv,
                  pl.BlockSpec(overlap.shape, lambda bi, gi, i: (0, 0)),
                  pl.BlockSpec(esel.shape, lambda bi, gi, i: (0, 0))],
        out_specs=[blk_q, blk_q, blk_q],
        out_shape=[out, out, out],
        scratch_shapes=[pltpu.VMEM((qb, t), F32)],
        compiler_params=_cparams(3),
        name="nsa_attention",
    )(q, kvc, k2, v2, overlap, esel)


def _oproj_kernel(x_ref, oc_ref, os_ref, ow_ref, gate_ref, e_ref, w_ref, gpost_ref, o_ref):
    d = oc_ref.shape[-1]
    gx = jnp.dot(gate_ref[...], e_ref[...], preferred_element_type=F32,
                 precision=lax.Precision.HIGHEST)
    o = gx[:, :d] * oc_ref[...] + gx[:, d:2 * d] * os_ref[...] + gx[:, 2 * d:] * ow_ref[...]
    m = _dot(o.astype(BF16), w_ref[...])
    o_ref[...] = x_ref[...] + _rms(m, gpost_ref[...])


def _oproj_call(x2d, oc, osel, ow, gates, expand, w_o, gpost, tm):
    n, d = x2d.shape
    dq = oc.shape[-1]
    row = lambda i: (i, 0)
    fix = lambda i: (0, 0)
    return pl.pallas_call(
        _oproj_kernel,
        grid=(n // tm,),
        in_specs=[pl.BlockSpec((tm, d), row), pl.BlockSpec((tm, dq), row), pl.BlockSpec((tm, dq), row),
                  pl.BlockSpec((tm, dq), row), pl.BlockSpec((tm, LANES), row),
                  pl.BlockSpec(expand.shape, fix), pl.BlockSpec(w_o.shape, fix), pl.BlockSpec((1, d), fix)],
        out_specs=pl.BlockSpec((tm, d), row),
        out_shape=jax.ShapeDtypeStruct((n, d), F32),
        compiler_params=_cparams(1),
        name="nsa_out_proj",
    )(x2d, oc, osel, ow, gates, expand, w_o, gpost.reshape(1, d))


def _shared_kv(x, kv_norm_g, w_kv, pe_k, pe_v, k_w1, k_w2, v_w1, v_w2):
    b, t, d = x.shape
    n = b * t
    g, hd = KV_HEADS, HEAD_DIM
    tm = _row_tile(t, 512)
    def cols(ta, tb):
        return np.concatenate([np.concatenate([np.arange(hd) + ta * g * hd + gi * hd,
                                               np.arange(hd) + tb * g * hd + gi * hd]) for gi in range(g)])
    k_cols = cols(2, 4)
    w_all = jnp.concatenate([w_kv[:, k_cols], w_kv[:, k_cols[_swap_perm(k_cols.size)]],
                             w_kv[:, cols(3, 5)], w_kv[:, cols(0, 1)]], axis=1).astype(BF16)
    cos, sin = _rope_tables(jnp.arange(t), LANES)
    k2, v2, ctok = _kv_call(x.reshape(n, d), kv_norm_g, w_all, cos, sin, t, tm)
    k2 = k2.reshape(b, t, g * 2 * hd)
    v2 = v2.reshape(b, t, g * 2 * hd)
    c = t // CMP_STRIDE
    n_cmp = c - CMP_BLOCK // CMP_STRIDE + 1
    assert CMP_BLOCK == 2 * CMP_STRIDE
    zc = ctok.reshape(b, c, CMP_STRIDE, g, 2, hd).transpose(0, 3, 4, 1, 2, 5).reshape(b, g, 2, c, CMP_STRIDE * hd)
    pe = jnp.stack([pe_k.reshape(2, CMP_STRIDE * hd), pe_v.reshape(2, CMP_STRIDE * hd)]).astype(F32)
    w1 = jnp.stack([k_w1, v_w1]).astype(BF16)
    zpad = jnp.zeros_like(k_w2)
    w2 = jnp.stack([jnp.concatenate([k_w2, zpad], axis=1),
                    jnp.concatenate([k_w2[:, _swap_perm(hd)], zpad], axis=1),
                    jnp.concatenate([zpad, v_w2], axis=1)]).astype(BF16)
    pos_c = jnp.arange(c) * CMP_STRIDE + CMP_BLOCK - 1
    cos_c, sin_c = _rope_tables(pos_c, hd)
    zlane = jnp.zeros((c, LANES - hd), F32)
    cos_c = jnp.concatenate([cos_c, zlane], axis=1)
    sin_c = jnp.concatenate([sin_c, zlane], axis=1)
    kvc = _cmp_call(zc, pe, w1, w2, cos_c, sin_c, n_cmp)
    return kvc, k2, v2, n_cmp


def _nsa_layer(x, gpre, gpost, w_q, w_o, kvc, k2, v2, n_cmp):
    b, t, d = x.shape
    n = b * t
    hd = HEAD_DIM
    dq = N_HEADS * hd
    tm = _row_tile(t, 512)
    wg = jnp.zeros((d, LANES), w_q.dtype).at[:, :3 * N_HEADS].set(w_q[:, dq:])
    w_all = jnp.concatenate([w_q[:, :dq], w_q[:, :dq][:, _swap_perm(dq)], wg], axis=1).astype(BF16)
    cos, sin = _rope_tables(jnp.arange(t), LANES)
    q, gates = _q_call(x.reshape(n, d), gpre, w_all, cos, sin, t, tm)
    n_sb = t // SEL_BLOCK
    c = kvc.shape[2]
    ci = np.arange(c)[:, None]
    sj = np.arange(LANES)[None, :]
    overlap = ((ci * CMP_STRIDE < (sj + 1) * SEL_BLOCK) & (ci * CMP_STRIDE + CMP_BLOCK > sj * SEL_BLOCK)
               & (ci < n_cmp) & (sj < n_sb)).astype(np.float32)
    esel = (np.arange(LANES)[:, None] == (np.arange(t)[None, :] // SEL_BLOCK)).astype(np.float32)
    oc, osel, ow = _attn_call(q.reshape(b, t, dq), kvc, k2, v2, jnp.asarray(overlap),
                              jnp.asarray(esel, dtype=BF16), n_sb)
    hh = np.arange(dq) // hd
    expand = np.zeros((LANES, 3 * dq), np.float32)
    for j in range(3):
        expand[hh * 3 + j, j * dq + np.arange(dq)] = 1.0
    out = _oproj_call(x.reshape(n, d), oc.reshape(n, dq), osel.reshape(n, dq), ow.reshape(n, dq), gates,
                      jnp.asarray(expand), w_o.astype(BF16), gpost, tm)
    return out.reshape(b, t, d)


def kernel(x, a_lam_re, a_lam_im, a_log_dt, a_b_re, a_b_im, a_c_re, a_c_im, a_d, a_w_glu, b_w_q, b_w_o, kv_norm_g, w_kv, cmp_pe_k, cmp_pe_v, cmp_k_w1, cmp_k_w2, cmp_v_w1, cmp_v_w2, mix_pre_g, mix_post_g, ffn_pre_g, ffn_post_g, ffn_w_in, ffn_conv_w, ffn_conv_b, ffn_w_out):
    depth = mix_pre_g.shape[0]
    n_a = depth // 2
    kv = None
    for layer in range(depth):
        if layer < n_a:
            i = layer
            x = _s5_layer(x, mix_pre_g[layer], mix_post_g[layer], a_lam_re[i], a_lam_im[i], a_log_dt[i],
                          a_b_re[i], a_b_im[i], a_c_re[i], a_c_im[i], a_d[i], a_w_glu[i])
        else:
            j = layer - n_a
            x = _nsa_layer(x, mix_pre_g[layer], mix_post_g[layer], b_w_q[j], b_w_o[j], *kv)
        x = _ffn_layer(x, ffn_pre_g[layer], ffn_post_g[layer], ffn_w_in[layer], ffn_conv_w[layer],
                       ffn_conv_b[layer], ffn_w_out[layer])
        if layer == n_a - 1:
            kv = _shared_kv(x, kv_norm_g, w_kv, cmp_pe_k, cmp_pe_v, cmp_k_w1, cmp_k_w2, cmp_v_w1, cmp_v_w2)
    return x
```

```python
import functools
import math

import numpy as np
import jax
import jax.numpy as jnp
from jax import lax
from jax.experimental import pallas as pl
from jax.experimental.pallas import tpu as pltpu

F32 = jnp.float32
BF16 = jnp.bfloat16

S5_GROUP = 16
S5_STATE = 64
N_HEADS = 16
KV_HEADS = 4
HEAD_DIM = 64
CMP_BLOCK = 32
CMP_STRIDE = 16
SEL_BLOCK = 64
SEL_TOPN = 16
WINDOW = 512
Q_BLOCK = 128
ROPE_THETA = 500000.0
ROPE_DIM = HEAD_DIM // 4
CONV_WIDTH = 3
EPS = 1e-6
NEG = -1e30
SEL_FORCE = 1e4
SEL_NEG = -1e4

LANES = 128
S5_CHUNK = 16
S5_GROUPS_PER_STEP = 8
CONV_HALO = 16
VMEM_LIMIT = 48 * 1024 * 1024


def _cparams(n_axes):
    return pltpu.CompilerParams(dimension_semantics=("arbitrary",) * n_axes,
                                vmem_limit_bytes=VMEM_LIMIT)


def _rms(x, g):
    return x * lax.rsqrt(jnp.mean(x * x, axis=-1, keepdims=True) + EPS) * g


def _gelu(x):
    return jax.nn.gelu(x, approximate=True)


def _dot(a, b):
    return jnp.dot(a, b, preferred_element_type=F32)


def _dot_nt(a, b):
    return lax.dot_general(a, b, (((1,), (1,)), ((), ())), preferred_element_type=F32)


def _row_tile(n, want):
    t = min(n, want)
    assert n % t == 0
    return t


def _s5_tables(lam_re, lam_im, log_dt, b_re, b_im, c_re, c_im, n_chunks):
    hp = lax.Precision.HIGHEST
    L = S5_CHUNK
    G, P = lam_re.shape
    I = b_re.shape[-1]
    dt = jnp.exp(log_dt.astype(F32))[:, None]
    lr, li = lam_re.astype(F32), lam_im.astype(F32)
    mag = jnp.exp(lr * dt)
    ab_re, ab_im = mag * jnp.cos(li * dt), mag * jnp.sin(li * dt)
    nr, ni = ab_re - 1.0, ab_im
    den = lr * lr + li * li
    coef_re = (nr * lr + ni * li) / den
    coef_im = (ni * lr - nr * li) / den
    br, bi = b_re.astype(F32), b_im.astype(F32)
    bb_re = coef_re[..., None] * br - coef_im[..., None] * bi
    bb_im = coef_re[..., None] * bi + coef_im[..., None] * br
    pr = [jnp.ones_like(ab_re)]
    pi = [jnp.zeros_like(ab_re)]
    for _ in range(L):
        r, i = pr[-1], pi[-1]
        pr.append(r * ab_re - i * ab_im)
        pi.append(r * ab_im + i * ab_re)
    pw_re = jnp.stack(pr)
    pw_im = jnp.stack(pi)
    cr, ci = c_re.astype(F32), c_im.astype(F32)
    cl_re = cr[None] * pw_re[:, :, None, :] - ci[None] * pw_im[:, :, None, :]
    cl_im = cr[None] * pw_im[:, :, None, :] + ci[None] * pw_re[:, :, None, :]
    kk = (jnp.einsum('kgop,gpi->gkio', cl_re[:L], bb_re, precision=hp)
          - jnp.einsum('kgop,gpi->gkio', cl_im[:L], bb_im, precision=hp))
    s_ix = np.arange(L)[:, None]
    r_ix = np.arange(L)[None, :]
    lag = np.maximum(r_ix - s_ix, 0)
    toep = kk[:, lag]
    toep = jnp.where(jnp.asarray(r_ix >= s_ix)[None, :, :, None, None], toep, 0.0)
    toep = toep.transpose(0, 1, 3, 2, 4).reshape(G, L * I, L * I)
    rev_re = pw_re[L - 1 - np.arange(L)]
    rev_im = pw_im[L - 1 - np.arange(L)]
    pb_re = rev_re[..., None] * bb_re[None] - rev_im[..., None] * bb_im[None]
    pb_im = rev_re[..., None] * bb_im[None] + rev_im[..., None] * bb_re[None]
    pmat = jnp.concatenate([pb_re.transpose(1, 0, 3, 2).reshape(G, L * I, P),
                            pb_im.transpose(1, 0, 3, 2).reshape(G, L * I, P)], axis=-1)
    w1 = jnp.concatenate([toep, pmat], axis=-1).astype(BF16)
    q_re = cl_re[1:].transpose(1, 3, 0, 2).reshape(G, P, L * I)
    q_im = cl_im[1:].transpose(1, 3, 0, 2).reshape(G, P, L * I)
    qmat = jnp.concatenate([q_re, -q_im], axis=1).astype(BF16)
    n_steps = max(1, int(math.ceil(math.log2(n_chunks)))) if n_chunks > 1 else 0
    mr, mi = pw_re[L], pw_im[L]
    a1, a2 = [], []
    for _ in range(max(n_steps, 1)):
        a1.append(jnp.concatenate([mr, mr], axis=-1))
        a2.append(jnp.concatenate([-mi, mi], axis=-1))
        mr, mi = mr * mr - mi * mi, 2.0 * mr * mi
    pad = (-len(a1)) % 8
    a1 = jnp.stack(a1 + [jnp.zeros_like(a1[0])] * pad, axis=1)
    a2 = jnp.stack(a2 + [jnp.zeros_like(a2[0])] * pad, axis=1)
    return w1, qmat, a1, a2, n_steps


def _norm_kernel(x_ref, g_ref, o_ref):
    o_ref[...] = _rms(x_ref[...], g_ref[...]).astype(o_ref.dtype)


def _norm_call(x2d, g, out_dtype, tm):
    n, d = x2d.shape
    return pl.pallas_call(
        _norm_kernel,
        grid=(n // tm,),
        in_specs=[pl.BlockSpec((tm, d), lambda i: (i, 0)),
                  pl.BlockSpec((1, d), lambda i: (0, 0))],
        out_specs=pl.BlockSpec((tm, d), lambda i: (i, 0)),
        out_shape=jax.ShapeDtypeStruct((n, d), out_dtype),
        compiler_params=_cparams(1),
        name="s5_prenorm",
    )(x2d, g.reshape(1, d))


def _s5_scan_kernel(u_ref, w1_ref, q_ref, a1_ref, a2_ref, y_ref, *, n_steps):
    gt, n_chunks = u_ref.shape[1], u_ref.shape[2]
    wid = S5_CHUNK * S5_GROUP
    row = lax.broadcasted_iota(jnp.int32, (n_chunks, 2 * S5_STATE), 0)
    for g in range(gt):
        r = _dot(u_ref[0, g], w1_ref[g])
        y_intra = r[:, :wid]
        x = r[:, wid:]
        for j in range(n_steps):
            k = 1 << j
            sh = jnp.where(row >= k, pltpu.roll(x, k, 0), 0.0)
            x = x + a1_ref[g, j:j + 1, :] * sh + a2_ref[g, j:j + 1, :] * pltpu.roll(sh, S5_STATE, 1)
        xp = jnp.where(row >= 1, pltpu.roll(x, 1, 0), 0.0)
        y_ref[0, g] = y_intra + _dot(xp.astype(BF16), q_ref[g])


def _s5_scan_call(uc, w1, qmat, a1, a2, n_steps):
    b, g, c, wid = uc.shape
    gt = min(S5_GROUPS_PER_STEP, g)
    return pl.pallas_call(
        functools.partial(_s5_scan_kernel, n_steps=n_steps),
        grid=(b, g // gt),
        in_specs=[pl.BlockSpec((1, gt, c, wid), lambda i, j: (i, j, 0, 0)),
                  pl.BlockSpec((gt,) + w1.shape[1:], lambda i, j: (j, 0, 0)),
                  pl.BlockSpec((gt,) + qmat.shape[1:], lambda i, j: (j, 0, 0)),
                  pl.BlockSpec((gt,) + a1.shape[1:], lambda i, j: (j, 0, 0)),
                  pl.BlockSpec((gt,) + a2.shape[1:], lambda i, j: (j, 0, 0))],
        out_specs=pl.BlockSpec((1, gt, c, wid), lambda i, j: (i, j, 0, 0)),
        out_shape=jax.ShapeDtypeStruct((b, g, c, wid), F32),
        compiler_params=_cparams(2),
        name="s5_scan",
    )(uc, w1, qmat, a1, a2)


def _s5_out_kernel(x_ref, y_ref, gpre_ref, d_ref, w_ref, gpost_ref, o_ref):
    x = x_ref[...]
    u = _rms(x, gpre_ref[...])
    z = _gelu(y_ref[...] + u * d_ref[...]).astype(BF16)
    ag = _dot(z, w_ref[...])
    d = x.shape[-1]
    m = ag[:, :d] * jax.nn.sigmoid(ag[:, d:])
    o_ref[...] = x + _rms(m, gpost_ref[...])


def _s5_out_call(x2d, y2d, gpre, dskip, wglu, gpost, tm):
    n, d = x2d.shape
    row = lambda i: (i, 0)
    fix = lambda i: (0, 0)
    return pl.pallas_call(
        _s5_out_kernel,
        grid=(n // tm,),
        in_specs=[pl.BlockSpec((tm, d), row), pl.BlockSpec((tm, d), row),
                  pl.BlockSpec((1, d), fix), pl.BlockSpec((1, d), fix),
                  pl.BlockSpec(wglu.shape, fix), pl.BlockSpec((1, d), fix)],
        out_specs=pl.BlockSpec((tm, d), row),
        out_shape=jax.ShapeDtypeStruct((n, d), F32),
        compiler_params=_cparams(1),
        name="s5_glu_out",
    )(x2d, y2d, gpre.reshape(1, d), dskip.reshape(1, d), wglu, gpost.reshape(1, d))


def _s5_layer(x, gpre, gpost, lam_re, lam_im, log_dt, b_re, b_im, c_re, c_im, d_skip, w_glu):
    b, t, d = x.shape
    g = d // S5_GROUP
    L = S5_CHUNK
    c = t // L
    n = b * t
    tm = _row_tile(n, 512)
    w1, qmat, a1, a2, n_steps = _s5_tables(lam_re, lam_im, log_dt, b_re, b_im, c_re, c_im, c)
    x2d = x.reshape(n, d)
    u = _norm_call(x2d, gpre, BF16, tm)
    uc = u.reshape(b, c, L, g, S5_GROUP).transpose(0, 3, 1, 2, 4).reshape(b, g, c, L * S5_GROUP)
    yc = _s5_scan_call(uc, w1, qmat, a1, a2, n_steps)
    y2d = yc.reshape(b, g, c, L, S5_GROUP).transpose(0, 2, 3, 1, 4).reshape(n, d)
    out = _s5_out_call(x2d, y2d, gpre, d_skip, w_glu.astype(BF16), gpost, tm)
    return out.reshape(b, t, d)


def _ffn_kernel(xprev_ref, x_ref, gpre_ref, wg_ref, wv_ref, cw_ref, cb_ref, wo_ref, gpost_ref,
                o_ref, xn_ref, acc_ref, *, seq_tiles):
    i = pl.program_id(0)
    j = pl.program_id(1)
    h = CONV_HALO

    @pl.when(j == 0)
    def _():
        g = gpre_ref[...]
        xn_ref[h:, :] = _rms(x_ref[...], g).astype(BF16)
        keep = jnp.where(i % seq_tiles == 0, 0.0, 1.0)
        xn_ref[:h, :] = (_rms(xprev_ref[...], g) * keep).astype(BF16)
        acc_ref[...] = jnp.zeros_like(acc_ref)

    xn = xn_ref[...]
    gate = _dot(xn, wg_ref[...])
    val = _dot(xn[h:], wv_ref[...])
    cw = cw_ref[...]
    conv = (cw[0:1] * pltpu.roll(gate, 2, 0) + cw[1:2] * pltpu.roll(gate, 1, 0)
            + cw[2:3] * gate + cb_ref[...])
    act = _gelu(conv[h:]) * val
    acc_ref[...] += _dot(act.astype(BF16), wo_ref[...])

    @pl.when(j == pl.num_programs(1) - 1)
    def _():
        o_ref[...] = x_ref[...] + _rms(acc_ref[...], gpost_ref[...])


def _ffn_layer(x, gpre, gpost, w_in, conv_w, conv_b, w_out):
    b, t, d = x.shape
    f = w_out.shape[0]
    n = b * t
    tm = _row_tile(t, 1024)
    tf = 256 if f % 256 == 0 else LANES
    assert f % tf == 0 and tm % CONV_HALO == 0
    nf = f // tf
    h = CONV_HALO
    x2d = x.reshape(n, d)
    cw = jnp.zeros((8, f), F32).at[:CONV_WIDTH].set(conv_w.astype(F32))
    w_in16 = w_in.astype(BF16)
    out = pl.pallas_call(
        functools.partial(_ffn_kernel, seq_tiles=t // tm),
        grid=(n // tm, nf),
        in_specs=[pl.BlockSpec((h, d), lambda i, j: (jnp.maximum(i * (tm // h) - 1, 0), 0)),
                  pl.BlockSpec((tm, d), lambda i, j: (i, 0)),
                  pl.BlockSpec((1, d), lambda i, j: (0, 0)),
                  pl.BlockSpec((d, tf), lambda i, j: (0, j)),
                  pl.BlockSpec((d, tf), lambda i, j: (0, nf + j)),
                  pl.BlockSpec((8, tf), lambda i, j: (0, j)),
                  pl.BlockSpec((1, tf), lambda i, j: (0, j)),
                  pl.BlockSpec((tf, d), lambda i, j: (j, 0)),
                  pl.BlockSpec((1, d), lambda i, j: (0, 0))],
        out_specs=pl.BlockSpec((tm, d), lambda i, j: (i, 0)),
        out_shape=jax.ShapeDtypeStruct((n, d), F32),
        scratch_shapes=[pltpu.VMEM((tm + h, d), BF16), pltpu.VMEM((tm, d), F32)],
        compiler_params=_cparams(2),
        name="conv_ffn",
    )(x2d, x2d, gpre.reshape(1, d), w_in16, w_in16, cw, conv_b.reshape(1, f).astype(F32),
      w_out.astype(BF16), gpost.reshape(1, d))
    return out.reshape(b, t, d)


def _rope_tables(pos, width):
    half = ROPE_DIM // 2
    inv = ROPE_THETA ** (-jnp.arange(half, dtype=F32) / half)
    ang = pos.astype(F32)[:, None] * inv[None, :]
    cos, sin = jnp.cos(ang), jnp.sin(ang)
    rest = HEAD_DIM - ROPE_DIM
    n = pos.shape[0]
    c = jnp.concatenate([cos, cos, jnp.ones((n, rest), F32)], axis=-1)
    s = jnp.concatenate([-sin, sin, jnp.zeros((n, rest), F32)], axis=-1)
    reps = width // HEAD_DIM
    return jnp.tile(c, (1, reps)), jnp.tile(s, (1, reps))


def _swap_perm(width):
    half = ROPE_DIM // 2
    d = np.arange(width)
    dd = d % HEAD_DIM
    return np.where(dd < half, d + half, np.where(dd < ROPE_DIM, d - half, d))


def _kv_kernel(x_ref, g_ref, wkt_ref, w_ref, cost_ref, sint_ref, kt_ref, v_ref, vs_ref, c_ref):
    wk = v_ref.shape[-1]
    sn = _rms(x_ref[...], g_ref[...]).astype(BF16)
    rt = _dot_nt(wkt_ref[...], sn)
    reps = wk // LANES
    cos = jnp.concatenate([cost_ref[...]] * reps, axis=0)
    sin = jnp.concatenate([sint_ref[...]] * reps, axis=0)
    kt_ref[0] = (rt[:wk] * cos + rt[wk:] * sin).astype(kt_ref.dtype)
    r = _dot(sn, w_ref[...])
    v_ref[...] = r[:, :wk].astype(v_ref.dtype)
    vs_ref[...] = r[:, wk:2 * wk].astype(vs_ref.dtype)
    c_ref[...] = r[:, 2 * wk:]


def _kv_call(x2d, g, wkt, w_rest, cos_t, sin_t, b, t, tm):
    n, d = x2d.shape
    wk = KV_HEADS * 2 * HEAD_DIM
    tpt = t // tm
    row = lambda i: (i, 0)
    fix = lambda i: (0, 0)
    tab = lambda i: (0, i % tpt)
    return pl.pallas_call(
        _kv_kernel,
        grid=(n // tm,),
        in_specs=[pl.BlockSpec((tm, d), row), pl.BlockSpec((1, d), fix),
                  pl.BlockSpec(wkt.shape, fix), pl.BlockSpec(w_rest.shape, fix),
                  pl.BlockSpec((LANES, tm), tab), pl.BlockSpec((LANES, tm), tab)],
        out_specs=[pl.BlockSpec((1, wk, tm), lambda i: (i // tpt, 0, i % tpt)),
                   pl.BlockSpec((tm, wk), row), pl.BlockSpec((tm, wk), row), pl.BlockSpec((tm, wk), row)],
        out_shape=[jax.ShapeDtypeStruct((b, wk, t), BF16), jax.ShapeDtypeStruct((n, wk), BF16),
                   jax.ShapeDtypeStruct((n, wk), BF16), jax.ShapeDtypeStruct((n, wk), F32)],
        compiler_params=_cparams(1),
        name="nsa_kv_proj",
    )(x2d, g.reshape(1, d), wkt, w_rest, cos_t, sin_t)


def _cmp_kernel(z_ref, pe_ref, w1_ref, w2kt_ref, w2v_ref, cost_ref, sint_ref, kt_ref, v_ref, *, n_cmp):
    c = z_ref.shape[3]
    half = z_ref.shape[-1]
    res = []
    for kv in range(2):
        z = z_ref[0, 0, kv]
        top = _dot((z + pe_ref[kv, 0:1, :]).astype(BF16), w1_ref[kv, :half, :])
        bot = _dot((z + pe_ref[kv, 1:2, :]).astype(BF16), w1_ref[kv, half:, :])
        hdn = _gelu(top + pltpu.roll(bot, c - 1, 0)).astype(BF16)
        res.append(hdn)
    kt = (_dot_nt(w2kt_ref[0], res[0]) * cost_ref[...] + _dot_nt(w2kt_ref[1], res[0]) * sint_ref[...])
    col = lax.broadcasted_iota(jnp.int32, kt.shape, 1)
    kt_ref[0, 0] = jnp.where(col < n_cmp, kt, 0.0).astype(kt_ref.dtype)
    v = _dot(res[1], w2v_ref[...])
    row = lax.broadcasted_iota(jnp.int32, v.shape, 0)
    v_ref[0, 0] = jnp.where(row < n_cmp, v, 0.0).astype(v_ref.dtype)


def _cmp_call(zc, pe, w1, w2kt, w2v, cos_t, sin_t, n_cmp):
    b, g, _, c, wid = zc.shape
    fix2 = lambda i, j: (0, 0)
    fix3 = lambda i, j: (0, 0, 0)
    return pl.pallas_call(
        functools.partial(_cmp_kernel, n_cmp=n_cmp),
        grid=(b, g),
        in_specs=[pl.BlockSpec((1, 1, 2, c, wid), lambda i, j: (i, j, 0, 0, 0)),
                  pl.BlockSpec(pe.shape, fix3), pl.BlockSpec(w1.shape, fix3),
                  pl.BlockSpec(w2kt.shape, fix3), pl.BlockSpec(w2v.shape, fix2),
                  pl.BlockSpec(cos_t.shape, fix2), pl.BlockSpec(sin_t.shape, fix2)],
        out_specs=[pl.BlockSpec((1, 1, LANES, c), lambda i, j: (i, j, 0, 0)),
                   pl.BlockSpec((1, 1, c, LANES), lambda i, j: (i, j, 0, 0))],
        out_shape=[jax.ShapeDtypeStruct((b, g, LANES, c), BF16), jax.ShapeDtypeStruct((b, g, c, LANES), BF16)],
        compiler_params=_cparams(2),
        name="nsa_compress",
    )(zc, pe, w1, w2kt, w2v, cos_t, sin_t)


def _q_kernel(x_ref, g_ref, w_ref, cos_ref, sin_ref, q_ref, gate_ref):
    d = q_ref.shape[-1]
    hn = _rms(x_ref[...], g_ref[...]).astype(BF16)
    r = _dot(hn, w_ref[...])
    reps = d // LANES
    cos = jnp.concatenate([cos_ref[...]] * reps, axis=1)
    sin = jnp.concatenate([sin_ref[...]] * reps, axis=1)
    q_ref[...] = ((r[:, :d] * cos + r[:, d:2 * d] * sin) * (HEAD_DIM ** -0.5)).astype(q_ref.dtype)
    gate_ref[...] = jax.nn.sigmoid(r[:, 2 * d:])


def _q_call(x2d, g, w_all, cos, sin, t, tm):
    n, d = x2d.shape
    dq = N_HEADS * HEAD_DIM
    tpt = t // tm
    row = lambda i: (i, 0)
    fix = lambda i: (0, 0)
    tab = lambda i: (i % tpt, 0)
    return pl.pallas_call(
        _q_kernel,
        grid=(n // tm,),
        in_specs=[pl.BlockSpec((tm, d), row), pl.BlockSpec((1, d), fix),
                  pl.BlockSpec(w_all.shape, fix),
                  pl.BlockSpec((tm, LANES), tab), pl.BlockSpec((tm, LANES), tab)],
        out_specs=[pl.BlockSpec((tm, dq), row), pl.BlockSpec((tm, LANES), row)],
        out_shape=[jax.ShapeDtypeStruct((n, dq), BF16), jax.ShapeDtypeStruct((n, LANES), F32)],
        compiler_params=_cparams(1),
        name="nsa_q_proj",
    )(x2d, g.reshape(1, d), w_all, cos, sin)


def _attn_kernel_v1(q_ref, kvc_ref, k2_ref, v2_ref, ov_ref, esel_ref, oc_ref, os_ref, ow_ref,
                    selexp_ref, *, n_sb, top_n):
    qb = q_ref.shape[1]
    rep = N_HEADS // KV_HEADS
    hd = HEAD_DIM
    i = pl.program_id(2)
    start = i * qb
    lane = lax.broadcasted_iota(jnp.int32, (qb, LANES), 1)
    lo = lane < hd

    q = q_ref[0].astype(F32)
    zero = jnp.zeros((qb, LANES), F32)
    q_lo, q_hi = [], []
    for p in range(rep * hd // LANES):
        part = q[:, p * LANES:(p + 1) * LANES]
        swapped = pltpu.roll(part, hd, 1)
        q_lo += [jnp.where(lo, part, zero), jnp.where(lo, swapped, zero)]
        q_hi += [jnp.where(lo, zero, swapped), jnp.where(lo, zero, part)]
    q_lo = jnp.concatenate(q_lo, axis=0).astype(BF16)
    q_hi = jnp.concatenate(q_hi, axis=0).astype(BF16)
    rows = rep * qb
    t_rows = start + (lax.broadcasted_iota(jnp.int32, (rows, LANES), 0) & (qb - 1))
    lane_r = lax.broadcasted_iota(jnp.int32, (rows, LANES), 1)

    def unstack(acc, valid_low):
        outs = []
        for p in range(rep // 2):
            a0 = acc[(2 * p) * qb:(2 * p + 1) * qb]
            a1 = acc[(2 * p + 1) * qb:(2 * p + 2) * qb]
            if valid_low:
                outs.append(jnp.where(lo, a0, pltpu.roll(a1, hd, 1)))
            else:
                outs.append(jnp.where(lo, pltpu.roll(a0, hd, 1), a1))
        return jnp.concatenate(outs, axis=1)

    kvc = kvc_ref[0, 0]
    n_c = kvc.shape[0]
    s1 = _dot_nt(q_lo, kvc)
    n_idx = lax.broadcasted_iota(jnp.int32, (rows, n_c), 1)
    t1 = start + (lax.broadcasted_iota(jnp.int32, (rows, n_c), 0) & (qb - 1))
    m1 = n_idx * CMP_STRIDE + (CMP_BLOCK - 1) <= t1
    s1m = jnp.where(m1, s1, NEG)
    e1 = jnp.where(m1, jnp.exp(s1m - jnp.max(s1m, axis=1, keepdims=True)), 0.0)
    l1 = jnp.sum(e1, axis=1, keepdims=True)
    p1 = e1 / jnp.where(l1 > 0.0, l1, 1.0)
    oc_ref[0] = unstack(_dot(p1.astype(BF16), kvc), False)

    psum = p1[0:qb]
    for r in range(1, rep):
        psum = psum + p1[r * qb:(r + 1) * qb]
    imp = jnp.dot(psum, ov_ref[...], preferred_element_type=F32, precision=lax.Precision.HIGHEST)
    t_q = start + lax.broadcasted_iota(jnp.int32, (qb, LANES), 0)
    cur = jnp.right_shift(t_q, int(math.log2(SEL_BLOCK)))
    valid = lane <= cur
    forced = (lane == 0) | (lane == cur) | (lane == cur - 1)
    score = jnp.where(forced, SEL_FORCE, jnp.where(valid, imp, SEL_NEG))
    score = jnp.where(lane < n_sb, score, -jnp.inf)
    lane_f = lane.astype(F32)
    work = score
    sel = jnp.zeros((qb, LANES), F32)
    for _ in range(top_n):
        mx = jnp.max(work, axis=1, keepdims=True)
        first = jnp.min(jnp.where(work == mx, lane_f, float(LANES)), axis=1, keepdims=True)
        hit = lane_f == first
        sel = jnp.where(hit, 1.0, sel)
        work = jnp.where(hit, -jnp.inf, work)
    sel = jnp.where(score > 0.5 * SEL_NEG, sel, 0.0)
    t_len = selexp_ref.shape[1]
    key = lax.broadcasted_iota(jnp.int32, (qb, t_len), 1)
    t_k = start + lax.broadcasted_iota(jnp.int32, (qb, t_len), 0)
    selexp_ref[...] = jnp.where(key <= t_k, _dot(sel.astype(BF16), esel_ref[...]), 0.0)

    def flash_step(qmat, kt, mask, carry):
        m, l, acc = carry
        off = pl.multiple_of(kt * LANES, LANES)
        s = _dot_nt(qmat, k2_ref[0, pl.ds(off, LANES), :])
        sm = jnp.where(mask, s, NEG)
        m_new = jnp.maximum(m, jnp.max(sm, axis=1, keepdims=True))
        alpha = jnp.exp(m - m_new)
        p = jnp.where(mask, jnp.exp(sm - m_new), 0.0)
        l = alpha * l + jnp.sum(p, axis=1, keepdims=True)
        acc = alpha * acc + _dot(p.astype(BF16), v2_ref[0, pl.ds(off, LANES), :])
        return m_new, l, acc

    init = (jnp.full((rows, 1), NEG, F32), jnp.zeros((rows, 1), F32), jnp.zeros((rows, LANES), F32))

    def sel_body(kt, carry):
        off = pl.multiple_of(kt * LANES, LANES)
        mk = selexp_ref[:, pl.ds(off, LANES)]
        mask = jnp.concatenate([mk] * rep, axis=0) > 0.5
        return flash_step(q_lo, kt, mask, carry)

    _, l2, acc2 = lax.fori_loop(0, i + 1, sel_body, init)
    os_ref[0] = unstack(acc2 / l2, True)

    def win_body(kt, carry):
        kpos = kt * LANES + lane_r
        mask = (kpos <= t_rows) & (t_rows - kpos < WINDOW)
        return flash_step(q_hi, kt, mask, carry)

    n_win = WINDOW // LANES
    _, l3, acc3 = lax.fori_loop(jnp.maximum(i - n_win, 0), i + 1, win_body, init)
    ow_ref[0] = unstack(acc3 / l3, False)


def _attn_call_v1(q, kvc, k2, v2, overlap, esel, n_sb):
    b, t, dq = q.shape
    g = KV_HEADS
    gw = dq // g
    c = kvc.shape[2]
    qb = Q_BLOCK
    assert qb == LANES and t % qb == 0
    blk_q = pl.BlockSpec((1, qb, gw), lambda bi, gi, i: (bi, i, gi))
    blk_kv = pl.BlockSpec((1, t, LANES), lambda bi, gi, i: (bi, 0, gi))
    out = jax.ShapeDtypeStruct((b, t, dq), F32)
    return pl.pallas_call(
        functools.partial(_attn_kernel, n_sb=n_sb, top_n=min(SEL_TOPN, n_sb)),
        grid=(b, g, t // qb),
        in_specs=[blk_q,
                  pl.BlockSpec((1, 1, c, LANES), lambda bi, gi, i: (bi, gi, 0, 0)),
                  blk_kv, blk_kv,
                  pl.BlockSpec(overlap.shape, lambda bi, gi, i: (0, 0)),
                  pl.BlockSpec(esel.shape, lambda bi, gi, i: (0, 0))],
        out_specs=[blk_q, blk_q, blk_q],
        out_shape=[out, out, out],
        scratch_shapes=[pltpu.VMEM((qb, t), F32)],
        compiler_params=_cparams(3),
        name="nsa_attention",
    )(q, kvc, k2, v2, overlap, esel)


ATTN_UNROLL = 4


def _attn_kernel(q_ref, kct_ref, vc_ref, kt_ref, v2_ref, v2s_ref, ovt_ref, esel_ref,
                 oc_ref, os_ref, ow_ref, kcbd_ref, vcbd_ref, kbd_ref, vbd_ref, msk_ref, sc_ref,
                 *, n_sb, top_n):
    qb = q_ref.shape[1]
    rep = N_HEADS // KV_HEADS
    hd = HEAD_DIM
    n_c = kct_ref.shape[-1]
    n_tiles = kt_ref.shape[-1] // LANES
    i = pl.program_id(2)
    start = i * qb

    @pl.when(i == 0)
    def _():
        kcbd_ref[...] = jnp.zeros_like(kcbd_ref)
        vcbd_ref[...] = jnp.zeros_like(vcbd_ref)
        kbd_ref[...] = jnp.zeros_like(kbd_ref)
        vbd_ref[...] = jnp.zeros_like(vbd_ref)
        vc = vc_ref[0, 0]
        for r in range(rep):
            h = hd * (r % 2)
            kcbd_ref[hd * r:hd * (r + 1), n_c * r:n_c * (r + 1)] = kct_ref[0, 0, 0:hd, :]
            vcbd_ref[n_c * r:n_c * (r + 1), hd * r:hd * (r + 1)] = vc[:, h:h + hd]

    qblk = q_ref[0]

    s1 = _dot(qblk, kcbd_ref[...])
    n_idx = lax.broadcasted_iota(jnp.int32, (qb, n_c), 1)
    t1 = start + lax.broadcasted_iota(jnp.int32, (qb, n_c), 0)
    m1 = n_idx * CMP_STRIDE + (CMP_BLOCK - 1) <= t1
    psum = jnp.zeros((qb, n_c), F32)
    p_parts = []
    for r in range(rep):
        sm = jnp.where(m1, s1[:, n_c * r:n_c * (r + 1)], NEG)
        e = jnp.where(m1, jnp.exp(sm - jnp.max(sm, axis=1, keepdims=True)), 0.0)
        l = jnp.sum(e, axis=1, keepdims=True)
        p = e / jnp.where(l > 0.0, l, 1.0)
        psum = psum + p
        p_parts.append(p.astype(BF16))
    oc_ref[0] = _dot(jnp.concatenate(p_parts, axis=1), vcbd_ref[...])

    p_hi = psum.astype(BF16)
    p_lo = (psum - p_hi.astype(F32)).astype(BF16)
    imp_t = _dot_nt(ovt_ref[...], p_hi) + _dot_nt(ovt_ref[...], p_lo)
    blk = lax.broadcasted_iota(jnp.int32, (n_sb, qb), 0)
    t_l = start + lax.broadcasted_iota(jnp.int32, (n_sb, qb), 1)
    cur = jnp.right_shift(t_l, int(math.log2(SEL_BLOCK)))
    forced = (blk == 0) | (blk == cur) | (blk == cur - 1)
    score = jnp.where(forced, SEL_FORCE, jnp.where(blk <= cur, imp_t[:n_sb], SEL_NEG))
    rank = jnp.zeros((n_sb, qb), F32)
    for jp in range(n_sb):
        other = score[jp:jp + 1, :]
        beats = (other > score) | ((other == score) & (blk > jp))
        rank = rank + jnp.where(beats, 1.0, 0.0)
    sel_t = jnp.where((rank < top_n) & (score > 0.5 * SEL_NEG), 1.0, 0.0)
    sel = sel_t.T.astype(BF16)

    lane = lax.broadcasted_iota(jnp.int32, (qb, LANES), 1)
    t_q = start + lax.broadcasted_iota(jnp.int32, (qb, LANES), 0)

    def scores(u, off, row0):
        ktile = kt_ref[0, row0:row0 + hd, pl.ds(off, LANES)]
        for r in range(rep):
            kbd_ref[u, hd * r:hd * (r + 1), LANES * r:LANES * (r + 1)] = ktile
        return _dot(qblk, kbd_ref[u])

    def weighted_values(u, off, ps, v_even, v_odd):
        ve = v_even[0, pl.ds(off, LANES), :]
        vo = v_odd[0, pl.ds(off, LANES), :]
        for r in range(rep):
            h = hd * (r % 2)
            src = ve if r % 2 == 0 else vo
            vbd_ref[u, LANES * r:LANES * (r + 1), hd * r:hd * (r + 1)] = src[:, h:h + hd]
        return _dot(jnp.concatenate(ps, axis=1), vbd_ref[u])

    def seg(a, r):
        return a[:, LANES * r:LANES * (r + 1)]

    def row_bcast(parts, op):
        return [jnp.broadcast_to(op(a, axis=1, keepdims=True), (qb, LANES)) for a in parts]

    def normalise(acc, ls):
        lrow = row_bcast(ls, jnp.sum)
        den = jnp.concatenate([jnp.where(lane < hd, lrow[2 * c], lrow[2 * c + 1]) for c in range(rep // 2)],
                              axis=1)
        return acc / den

    n_w = WINDOW // LANES + 1

    def win_mask(w):
        kpos = (i - (n_w - 1) + w) * LANES + lane
        return (kpos >= 0) & (kpos <= t_q) & (t_q - kpos < WINDOW)

    w_off = [pl.multiple_of(jnp.maximum(i - (n_w - 1) + w, 0) * LANES, LANES) for w in range(n_w)]
    s_w = [scores(w, w_off[w], hd) for w in range(n_w)]
    mx = [jnp.full((qb, LANES), NEG, F32) for _ in range(rep)]
    for w in range(n_w):
        mk = win_mask(w)
        mx = [jnp.maximum(mx[r], jnp.where(mk, seg(s_w[w], r), NEG)) for r in range(rep)]
    mrow = row_bcast(mx, jnp.max)
    ls = [jnp.zeros((qb, LANES), F32) for _ in range(rep)]
    acc = jnp.zeros((qb, rep * hd), F32)
    for w in range(n_w):
        mk = win_mask(w)
        ps = [jnp.where(mk, jnp.exp(seg(s_w[w], r) - mrow[r]), 0.0) for r in range(rep)]
        ls = [ls[r] + ps[r] for r in range(rep)]
        acc = acc + weighted_values(w, w_off[w], [p.astype(BF16) for p in ps], v2s_ref, v2_ref)
    ow_ref[0] = normalise(acc, ls)

    n_it = lax.div(i + ATTN_UNROLL, ATTN_UNROLL)

    def tile(p, u):
        kt = p * ATTN_UNROLL + u
        return kt, pl.multiple_of(jnp.minimum(kt, n_tiles - 1) * LANES, LANES), pl.multiple_of(kt * LANES, LANES)

    def pass1(p, mx):
        mx = list(mx)
        for u in range(ATTN_UNROLL):
            kt, off, slot = tile(p, u)
            s = scores(u, off, 0)
            sc_ref[kt] = s
            picked = _dot(sel, esel_ref[0:n_sb, pl.ds(off, LANES)])
            mk = jnp.where(kt * LANES + lane <= t_q, picked, 0.0)
            msk_ref[:, pl.ds(slot, LANES)] = mk
            mx = [jnp.maximum(mx[r], jnp.where(mk > 0.5, seg(s, r), NEG)) for r in range(rep)]
        return tuple(mx)

    mx = lax.fori_loop(0, n_it, pass1, tuple(jnp.full((qb, LANES), NEG, F32) for _ in range(rep)))
    mrow = row_bcast(mx, jnp.max)

    def pass2(p, carry):
        ls, acc = carry
        for u in range(ATTN_UNROLL):
            kt, off, slot = tile(p, u)
            s = sc_ref[kt]
            mk = msk_ref[:, pl.ds(slot, LANES)] > 0.5
            ps = [jnp.where(mk, jnp.exp(seg(s, r) - mrow[r]), 0.0) for r in range(rep)]
            ls = tuple(ls[r] + ps[r] for r in range(rep))
            acc = acc + weighted_values(u, off, [p.astype(BF16) for p in ps], v2_ref, v2s_ref)
        return ls, acc

    ls, acc = lax.fori_loop(0, n_it, pass2, (tuple(jnp.zeros((qb, LANES), F32) for _ in range(rep)),
                                             jnp.zeros((qb, rep * hd), F32)))
    os_ref[0] = normalise(acc, ls)


def _attn_call(q, kct, vc, kt, v2, v2s, overlap_t, esel, n_sb):
    b, t, dq = q.shape
    g = KV_HEADS
    gw = dq // g
    rep = N_HEADS // KV_HEADS
    c = kct.shape[-1]
    qb = Q_BLOCK
    assert qb == LANES and t % qb == 0 and rep % 2 == 0 and gw == 2 * LANES
    blk_q = pl.BlockSpec((1, qb, gw), lambda bi, gi, i: (bi, i, gi))
    blk_v = pl.BlockSpec((1, t, LANES), lambda bi, gi, i: (bi, 0, gi))
    fix2 = lambda bi, gi, i: (0, 0)
    out = jax.ShapeDtypeStruct((b, t, dq), F32)
    n_slots = t // LANES + ATTN_UNROLL
    n_bd = max(ATTN_UNROLL, WINDOW // LANES + 1)
    return pl.pallas_call(
        functools.partial(_attn_kernel, n_sb=n_sb, top_n=min(SEL_TOPN, n_sb)),
        grid=(b, g, t // qb),
        in_specs=[blk_q,
                  pl.BlockSpec((1, 1, LANES, c), lambda bi, gi, i: (bi, gi, 0, 0)),
                  pl.BlockSpec((1, 1, c, LANES), lambda bi, gi, i: (bi, gi, 0, 0)),
                  pl.BlockSpec((1, LANES, t), lambda bi, gi, i: (bi, gi, 0)),
                  blk_v, blk_v,
                  pl.BlockSpec(overlap_t.shape, fix2), pl.BlockSpec(esel.shape, fix2)],
        out_specs=[blk_q, blk_q, blk_q],
        out_shape=[out, out, out],
        scratch_shapes=[pltpu.VMEM((rep * HEAD_DIM, rep * c), BF16),
                        pltpu.VMEM((rep * c, rep * HEAD_DIM), BF16),
                        pltpu.VMEM((n_bd, rep * HEAD_DIM, rep * LANES), BF16),
                        pltpu.VMEM((n_bd, rep * LANES, rep * HEAD_DIM), BF16),
                        pltpu.VMEM((qb, n_slots * LANES), F32),
                        pltpu.VMEM((n_slots, qb, rep * LANES), F32)],
        compiler_params=_cparams(3),
        name="nsa_attention",
    )(q, kct, vc, kt, v2, v2s, overlap_t, esel)


def _oproj_kernel(x_ref, oc_ref, os_ref, ow_ref, gate_ref, e_ref, w_ref, gpost_ref, o_ref):
    d = oc_ref.shape[-1]
    gx = jnp.dot(gate_ref[...], e_ref[...], preferred_element_type=F32,
                 precision=lax.Precision.HIGHEST)
    o = gx[:, :d] * oc_ref[...] + gx[:, d:2 * d] * os_ref[...] + gx[:, 2 * d:] * ow_ref[...]
    m = _dot(o.astype(BF16), w_ref[...])
    o_ref[...] = x_ref[...] + _rms(m, gpost_ref[...])


def _oproj_call(x2d, oc, osel, ow, gates, expand, w_o, gpost, tm):
    n, d = x2d.shape
    dq = oc.shape[-1]
    row = lambda i: (i, 0)
    fix = lambda i: (0, 0)
    return pl.pallas_call(
        _oproj_kernel,
        grid=(n // tm,),
        in_specs=[pl.BlockSpec((tm, d), row), pl.BlockSpec((tm, dq), row), pl.BlockSpec((tm, dq), row),
                  pl.BlockSpec((tm, dq), row), pl.BlockSpec((tm, LANES), row),
                  pl.BlockSpec(expand.shape, fix), pl.BlockSpec(w_o.shape, fix), pl.BlockSpec((1, d), fix)],
        out_specs=pl.BlockSpec((tm, d), row),
        out_shape=jax.ShapeDtypeStruct((n, d), F32),
        compiler_params=_cparams(1),
        name="nsa_out_proj",
    )(x2d, oc, osel, ow, gates, expand, w_o, gpost.reshape(1, d))


def _shared_kv(x, kv_norm_g, w_kv, pe_k, pe_v, k_w1, k_w2, v_w1, v_w2):
    b, t, d = x.shape
    n = b * t
    g, hd = KV_HEADS, HEAD_DIM
    tm = _row_tile(t, 512)
    def cols(ta, tb):
        return np.concatenate([np.concatenate([np.arange(hd) + ta * g * hd + gi * hd,
                                               np.arange(hd) + tb * g * hd + gi * hd]) for gi in range(g)])
    k_cols = cols(2, 4)
    wkt = jnp.concatenate([w_kv[:, k_cols], w_kv[:, k_cols[_swap_perm(k_cols.size)]]], axis=1).T.astype(BF16)
    w_rest = jnp.concatenate([w_kv[:, cols(3, 5)], w_kv[:, cols(5, 3)], w_kv[:, cols(0, 1)]], axis=1).astype(BF16)
    cos, sin = _rope_tables(jnp.arange(t), LANES)
    kt, v2, v2s, ctok = _kv_call(x.reshape(n, d), kv_norm_g, wkt, w_rest, cos.T, sin.T, b, t, tm)
    v2 = v2.reshape(b, t, g * 2 * hd)
    v2s = v2s.reshape(b, t, g * 2 * hd)
    c = t // CMP_STRIDE
    n_cmp = c - CMP_BLOCK // CMP_STRIDE + 1
    assert CMP_BLOCK == 2 * CMP_STRIDE
    zc = ctok.reshape(b, c, CMP_STRIDE, g, 2, hd).transpose(0, 3, 4, 1, 2, 5).reshape(b, g, 2, c, CMP_STRIDE * hd)
    pe = jnp.stack([pe_k.reshape(2, CMP_STRIDE * hd), pe_v.reshape(2, CMP_STRIDE * hd)]).astype(F32)
    w1 = jnp.stack([k_w1, v_w1]).astype(BF16)
    zpad = jnp.zeros_like(k_w2)
    w2kt = jnp.stack([jnp.concatenate([k_w2, zpad], axis=1).T,
                      jnp.concatenate([k_w2[:, _swap_perm(hd)], zpad], axis=1).T]).astype(BF16)
    w2v = jnp.concatenate([v_w2, v_w2], axis=1).astype(BF16)
    pos_c = jnp.arange(c) * CMP_STRIDE + CMP_BLOCK - 1
    cos_c, sin_c = _rope_tables(pos_c, hd)
    zlane = jnp.zeros((c, LANES - hd), F32)
    cos_c = jnp.concatenate([cos_c, zlane], axis=1).T
    sin_c = jnp.concatenate([sin_c, zlane], axis=1).T
    kct, vc = _cmp_call(zc, pe, w1, w2kt, w2v, cos_c, sin_c, n_cmp)
    return kct, vc, kt, v2, v2s, n_cmp


def _nsa_layer(x, gpre, gpost, w_q, w_o, kct, vc, kt, v2, v2s, n_cmp):
    b, t, d = x.shape
    n = b * t
    hd = HEAD_DIM
    dq = N_HEADS * hd
    tm = _row_tile(t, 512)
    wg = jnp.zeros((d, LANES), w_q.dtype).at[:, :3 * N_HEADS].set(w_q[:, dq:])
    w_all = jnp.concatenate([w_q[:, :dq], w_q[:, :dq][:, _swap_perm(dq)], wg], axis=1).astype(BF16)
    cos, sin = _rope_tables(jnp.arange(t), LANES)
    q, gates = _q_call(x.reshape(n, d), gpre, w_all, cos, sin, t, tm)
    n_sb = t // SEL_BLOCK
    assert n_sb % 8 == 0 and n_sb <= LANES
    c = vc.shape[2]
    ci = np.arange(c)[None, :]
    sj = np.arange(LANES)[:, None]
    overlap_t = ((ci * CMP_STRIDE < (sj + 1) * SEL_BLOCK) & (ci * CMP_STRIDE + CMP_BLOCK > sj * SEL_BLOCK)
                 & (ci < n_cmp) & (sj < n_sb)).astype(np.float32)
    esel = (np.arange(LANES)[:, None] == (np.arange(t)[None, :] // SEL_BLOCK)).astype(np.float32)
    oc, osel, ow = _attn_call(q.reshape(b, t, dq), kct, vc, kt, v2, v2s, jnp.asarray(overlap_t, dtype=BF16),
                              jnp.asarray(esel, dtype=BF16), n_sb)
    hh = np.arange(dq) // hd
    expand = np.zeros((LANES, 3 * dq), np.float32)
    for j in range(3):
        expand[hh * 3 + j, j * dq + np.arange(dq)] = 1.0
    out = _oproj_call(x.reshape(n, d), oc.reshape(n, dq), osel.reshape(n, dq), ow.reshape(n, dq), gates,
                      jnp.asarray(expand), w_o.astype(BF16), gpost, tm)
    return out.reshape(b, t, d)


def kernel(x, a_lam_re, a_lam_im, a_log_dt, a_b_re, a_b_im, a_c_re, a_c_im, a_d, a_w_glu, b_w_q, b_w_o, kv_norm_g, w_kv, cmp_pe_k, cmp_pe_v, cmp_k_w1, cmp_k_w2, cmp_v_w1, cmp_v_w2, mix_pre_g, mix_post_g, ffn_pre_g, ffn_post_g, ffn_w_in, ffn_conv_w, ffn_conv_b, ffn_w_out):
    depth = mix_pre_g.shape[0]
    n_a = depth // 2
    kv = None
    for layer in range(depth):
        if layer < n_a:
            i = layer
            x = _s5_layer(x, mix_pre_g[layer], mix_post_g[layer], a_lam_re[i], a_lam_im[i], a_log_dt[i],
                          a_b_re[i], a_b_im[i], a_c_re[i], a_c_im[i], a_d[i], a_w_glu[i])
        else:
            j = layer - n_a
            x = _nsa_layer(x, mix_pre_g[layer], mix_post_g[layer], b_w_q[j], b_w_o[j], *kv)
        x = _ffn_layer(x, ffn_pre_g[layer], ffn_post_g[layer], ffn_w_in[layer], ffn_conv_w[layer],
                       ffn_conv_b[layer], ffn_w_out[layer])
        if layer == n_a - 1:
            kv = _shared_kv(x, kv_norm_g, w_kv, cmp_pe_k, cmp_pe_v, cmp_k_w1, cmp_k_w2, cmp_v_w1, cmp_v_w2)
    return x
```

```python
import functools
import math

import numpy as np
import jax
import jax.numpy as jnp
from jax import lax
from jax.experimental import pallas as pl
from jax.experimental.pallas import tpu as pltpu

F32 = jnp.float32
BF16 = jnp.bfloat16

S5_GROUP = 16
S5_STATE = 64
N_HEADS = 16
KV_HEADS = 4
HEAD_DIM = 64
CMP_BLOCK = 32
CMP_STRIDE = 16
SEL_BLOCK = 64
SEL_TOPN = 16
WINDOW = 512
Q_BLOCK = 128
ROPE_THETA = 500000.0
ROPE_DIM = HEAD_DIM // 4
CONV_WIDTH = 3
EPS = 1e-6
NEG = -1e30
SEL_FORCE = 1e4
SEL_NEG = -1e4
LOG2E = math.log2(math.e)

LANES = 128
S5_CHUNK = 16
S5_GROUPS_PER_STEP = 8
CONV_HALO = 16
VMEM_LIMIT = 48 * 1024 * 1024


def _cparams(n_axes):
    return pltpu.CompilerParams(dimension_semantics=("arbitrary",) * n_axes,
                                vmem_limit_bytes=VMEM_LIMIT)


def _rms(x, g):
    return x * lax.rsqrt(jnp.mean(x * x, axis=-1, keepdims=True) + EPS) * g


def _gelu(x):
    return jax.nn.gelu(x, approximate=True)


def _dot(a, b):
    return jnp.dot(a, b, preferred_element_type=F32)


def _dot_nt(a, b):
    return lax.dot_general(a, b, (((1,), (1,)), ((), ())), preferred_element_type=F32)


def _row_tile(n, want):
    t = min(n, want)
    assert n % t == 0
    return t


def _s5_tables(lam_re, lam_im, log_dt, b_re, b_im, c_re, c_im, n_chunks):
    hp = lax.Precision.HIGHEST
    L = S5_CHUNK
    G, P = lam_re.shape
    I = b_re.shape[-1]
    dt = jnp.exp(log_dt.astype(F32))[:, None]
    lr, li = lam_re.astype(F32), lam_im.astype(F32)
    mag = jnp.exp(lr * dt)
    ab_re, ab_im = mag * jnp.cos(li * dt), mag * jnp.sin(li * dt)
    nr, ni = ab_re - 1.0, ab_im
    den = lr * lr + li * li
    coef_re = (nr * lr + ni * li) / den
    coef_im = (ni * lr - nr * li) / den
    br, bi = b_re.astype(F32), b_im.astype(F32)
    bb_re = coef_re[..., None] * br - coef_im[..., None] * bi
    bb_im = coef_re[..., None] * bi + coef_im[..., None] * br
    pr = [jnp.ones_like(ab_re)]
    pi = [jnp.zeros_like(ab_re)]
    for _ in range(L):
        r, i = pr[-1], pi[-1]
        pr.append(r * ab_re - i * ab_im)
        pi.append(r * ab_im + i * ab_re)
    pw_re = jnp.stack(pr)
    pw_im = jnp.stack(pi)
    cr, ci = c_re.astype(F32), c_im.astype(F32)
    cl_re = cr[None] * pw_re[:, :, None, :] - ci[None] * pw_im[:, :, None, :]
    cl_im = cr[None] * pw_im[:, :, None, :] + ci[None] * pw_re[:, :, None, :]
    kk = (jnp.einsum('kgop,gpi->gkio', cl_re[:L], bb_re, precision=hp)
          - jnp.einsum('kgop,gpi->gkio', cl_im[:L], bb_im, precision=hp))
    s_ix = np.arange(L)[:, None]
    r_ix = np.arange(L)[None, :]
    lag = np.maximum(r_ix - s_ix, 0)
    toep = kk[:, lag]
    toep = jnp.where(jnp.asarray(r_ix >= s_ix)[None, :, :, None, None], toep, 0.0)
    toep = toep.transpose(0, 1, 3, 2, 4).reshape(G, L * I, L * I)
    rev_re = pw_re[L - 1 - np.arange(L)]
    rev_im = pw_im[L - 1 - np.arange(L)]
    pb_re = rev_re[..., None] * bb_re[None] - rev_im[..., None] * bb_im[None]
    pb_im = rev_re[..., None] * bb_im[None] + rev_im[..., None] * bb_re[None]
    pmat = jnp.concatenate([pb_re.transpose(1, 0, 3, 2).reshape(G, L * I, P),
                            pb_im.transpose(1, 0, 3, 2).reshape(G, L * I, P)], axis=-1)
    q_re = cl_re[1:].transpose(1, 3, 0, 2).reshape(G, P, L * I)
    q_im = cl_im[1:].transpose(1, 3, 0, 2).reshape(G, P, L * I)
    qmat = jnp.concatenate([q_re, -q_im], axis=1)
    gl = LANES // I
    nt = G // gl
    eye = jnp.eye(gl, dtype=F32)
    wy = jnp.einsum('tgsiro,gh->tsgirho', toep.reshape(nt, gl, L, I, L, I), eye).reshape(nt, L * LANES, L * LANES)
    ws = jnp.einsum('tgsip,gh->tsgihp', pmat.reshape(nt, gl, L, I, 2 * P), eye).reshape(nt, L * LANES, gl * 2 * P)
    wbig = jnp.concatenate([wy, ws], axis=-1).astype(BF16)
    qbig = jnp.einsum('tgpro,gh->tgprho', qmat.reshape(nt, gl, 2 * P, L, I), eye)
    qbig = qbig.reshape(nt, gl * 2 * P, L * LANES).astype(BF16)
    n_steps = int(math.ceil(math.log2(n_chunks))) if n_chunks > 1 else 0
    mr, mi = pw_re[L], pw_im[L]
    a1, a2 = [], []
    for _ in range(max(n_steps, 1)):
        a1.append(jnp.concatenate([mr, mr], axis=-1).reshape(nt, gl * 2 * P))
        a2.append(jnp.concatenate([-mi, mi], axis=-1).reshape(nt, gl * 2 * P))
        mr, mi = mr * mr - mi * mi, 2.0 * mr * mi
    pad = (-len(a1)) % 8
    a1 = jnp.stack(a1 + [jnp.zeros_like(a1[0])] * pad, axis=1)
    a2 = jnp.stack(a2 + [jnp.zeros_like(a2[0])] * pad, axis=1)
    return wbig, qbig, a1, a2, n_steps


def _norm_kernel(x_ref, g_ref, o_ref):
    o_ref[...] = _rms(x_ref[...], g_ref[...]).astype(o_ref.dtype)


def _norm_call(x2d, g, out_dtype, tm):
    n, d = x2d.shape
    return pl.pallas_call(
        _norm_kernel,
        grid=(n // tm,),
        in_specs=[pl.BlockSpec((tm, d), lambda i: (i, 0)),
                  pl.BlockSpec((1, d), lambda i: (0, 0))],
        out_specs=pl.BlockSpec((tm, d), lambda i: (i, 0)),
        out_shape=jax.ShapeDtypeStruct((n, d), out_dtype),
        compiler_params=_cparams(1),
        name="s5_prenorm",
    )(x2d, g.reshape(1, d))


def _s5_scan_kernel(u_ref, w_ref, q_ref, a1_ref, a2_ref, y_ref, *, n_steps):
    L = S5_CHUNK
    n_chunks = u_ref.shape[1] // L
    wy = L * LANES
    ucat = jnp.concatenate([u_ref[0, pl.ds(s, n_chunks, stride=L), :] for s in range(L)], axis=1).astype(BF16)
    r = _dot(ucat, w_ref[0])
    row = lax.broadcasted_iota(jnp.int32, (n_chunks, LANES), 0)
    xprev = []
    for g in range(q_ref.shape[1] // LANES):
        x = r[:, wy + LANES * g:wy + LANES * (g + 1)]
        for j in range(n_steps):
            k = 1 << j
            sh = jnp.where(row >= k, pltpu.roll(x, k, 0), 0.0)
            x = (x + a1_ref[0, j:j + 1, LANES * g:LANES * (g + 1)] * sh
                 + a2_ref[0, j:j + 1, LANES * g:LANES * (g + 1)] * pltpu.roll(sh, S5_STATE, 1))
        xprev.append(jnp.where(row >= 1, pltpu.roll(x, 1, 0), 0.0).astype(BF16))
    y = r[:, :wy] + _dot(jnp.concatenate(xprev, axis=1), q_ref[0])
    for s in range(L):
        y_ref[0, pl.ds(s, n_chunks, stride=L), :] = y[:, LANES * s:LANES * (s + 1)]


def _s5_scan_call(u, wbig, qbig, a1, a2, n_steps):
    b, t, d = u.shape
    nt = d // LANES
    once = pl.Buffered(1)
    return pl.pallas_call(
        functools.partial(_s5_scan_kernel, n_steps=n_steps),
        grid=(nt, b),
        in_specs=[pl.BlockSpec((1, t, LANES), lambda j, i: (i, 0, j)),
                  pl.BlockSpec((1,) + wbig.shape[1:], lambda j, i: (j, 0, 0), pipeline_mode=once),
                  pl.BlockSpec((1,) + qbig.shape[1:], lambda j, i: (j, 0, 0), pipeline_mode=once),
                  pl.BlockSpec((1,) + a1.shape[1:], lambda j, i: (j, 0, 0)),
                  pl.BlockSpec((1,) + a2.shape[1:], lambda j, i: (j, 0, 0))],
        out_specs=pl.BlockSpec((1, t, LANES), lambda j, i: (i, 0, j)),
        out_shape=jax.ShapeDtypeStruct((b, t, d), F32),
        compiler_params=_cparams(2),
        name="s5_scan",
    )(u, wbig, qbig, a1, a2)


def _s5_out_kernel(x_ref, y_ref, gpre_ref, d_ref, w_ref, gpost_ref, o_ref):
    x = x_ref[...]
    u = _rms(x, gpre_ref[...])
    z = _gelu(y_ref[...] + u * d_ref[...]).astype(BF16)
    ag = _dot(z, w_ref[...])
    d = x.shape[-1]
    m = ag[:, :d] * jax.nn.sigmoid(ag[:, d:])
    o_ref[...] = x + _rms(m, gpost_ref[...])


def _s5_out_call(x2d, y2d, gpre, dskip, wglu, gpost, tm):
    n, d = x2d.shape
    row = lambda i: (i, 0)
    fix = lambda i: (0, 0)
    return pl.pallas_call(
        _s5_out_kernel,
        grid=(n // tm,),
        in_specs=[pl.BlockSpec((tm, d), row), pl.BlockSpec((tm, d), row),
                  pl.BlockSpec((1, d), fix), pl.BlockSpec((1, d), fix),
                  pl.BlockSpec(wglu.shape, fix), pl.BlockSpec((1, d), fix)],
        out_specs=pl.BlockSpec((tm, d), row),
        out_shape=jax.ShapeDtypeStruct((n, d), F32),
        compiler_params=_cparams(1),
        name="s5_glu_out",
    )(x2d, y2d, gpre.reshape(1, d), dskip.reshape(1, d), wglu, gpost.reshape(1, d))


def _s5_layer(x, gpre, gpost, lam_re, lam_im, log_dt, b_re, b_im, c_re, c_im, d_skip, w_glu):
    b, t, d = x.shape
    n = b * t
    tm = _row_tile(n, 512)
    assert t % (8 * S5_CHUNK) == 0 and d % LANES == 0 and LANES % S5_GROUP == 0
    wbig, qbig, a1, a2, n_steps = _s5_tables(lam_re, lam_im, log_dt, b_re, b_im, c_re, c_im, t // S5_CHUNK)
    x2d = x.reshape(n, d)
    u = _norm_call(x2d, gpre, F32, tm)
    y = _s5_scan_call(u.reshape(b, t, d), wbig, qbig, a1, a2, n_steps)
    out = _s5_out_call(x2d, y.reshape(n, d), gpre, d_skip, w_glu.astype(BF16), gpost, tm)
    return out.reshape(b, t, d)


def _ffn_kernel(xprev_ref, x_ref, gpre_ref, wg_ref, wv_ref, cw_ref, cb_ref, wo_ref, gpost_ref,
                o_ref, xn_ref, acc_ref, *, seq_tiles):
    i = pl.program_id(0)
    j = pl.program_id(1)
    h = CONV_HALO

    @pl.when(j == 0)
    def _():
        g = gpre_ref[...]
        xn_ref[h:, :] = _rms(x_ref[...], g).astype(BF16)
        keep = jnp.where(i % seq_tiles == 0, 0.0, 1.0)
        xn_ref[:h, :] = (_rms(xprev_ref[...], g) * keep).astype(BF16)
        acc_ref[...] = jnp.zeros_like(acc_ref)

    xn = xn_ref[...]
    gate = _dot(xn, wg_ref[...])
    val = _dot(xn[h:], wv_ref[...])
    cw = cw_ref[...]
    conv = (cw[0:1] * pltpu.roll(gate, 2, 0) + cw[1:2] * pltpu.roll(gate, 1, 0)
            + cw[2:3] * gate + cb_ref[...])
    act = _gelu(conv[h:]) * val
    acc_ref[...] += _dot(act.astype(BF16), wo_ref[...])

    @pl.when(j == pl.num_programs(1) - 1)
    def _():
        o_ref[...] = x_ref[...] + _rms(acc_ref[...], gpost_ref[...])


def _ffn_layer(x, gpre, gpost, w_in, conv_w, conv_b, w_out):
    b, t, d = x.shape
    f = w_out.shape[0]
    n = b * t
    tm = _row_tile(t, 1024)
    tf = 256 if f % 256 == 0 else LANES
    assert f % tf == 0 and tm % CONV_HALO == 0
    nf = f // tf
    h = CONV_HALO
    x2d = x.reshape(n, d)
    cw = jnp.zeros((8, f), F32).at[:CONV_WIDTH].set(conv_w.astype(F32))
    w_in16 = w_in.astype(BF16)
    out = pl.pallas_call(
        functools.partial(_ffn_kernel, seq_tiles=t // tm),
        grid=(n // tm, nf),
        in_specs=[pl.BlockSpec((h, d), lambda i, j: (jnp.maximum(i * (tm // h) - 1, 0), 0)),
                  pl.BlockSpec((tm, d), lambda i, j: (i, 0)),
                  pl.BlockSpec((1, d), lambda i, j: (0, 0)),
                  pl.BlockSpec((d, tf), lambda i, j: (0, j)),
                  pl.BlockSpec((d, tf), lambda i, j: (0, nf + j)),
                  pl.BlockSpec((8, tf), lambda i, j: (0, j)),
                  pl.BlockSpec((1, tf), lambda i, j: (0, j)),
                  pl.BlockSpec((tf, d), lambda i, j: (j, 0)),
                  pl.BlockSpec((1, d), lambda i, j: (0, 0))],
        out_specs=pl.BlockSpec((tm, d), lambda i, j: (i, 0)),
        out_shape=jax.ShapeDtypeStruct((n, d), F32),
        scratch_shapes=[pltpu.VMEM((tm + h, d), BF16), pltpu.VMEM((tm, d), F32)],
        compiler_params=_cparams(2),
        name="conv_ffn",
    )(x2d, x2d, gpre.reshape(1, d), w_in16, w_in16, cw, conv_b.reshape(1, f).astype(F32),
      w_out.astype(BF16), gpost.reshape(1, d))
    return out.reshape(b, t, d)


def _rope_tables(pos, width):
    half = ROPE_DIM // 2
    inv = ROPE_THETA ** (-jnp.arange(half, dtype=F32) / half)
    ang = pos.astype(F32)[:, None] * inv[None, :]
    cos, sin = jnp.cos(ang), jnp.sin(ang)
    rest = HEAD_DIM - ROPE_DIM
    n = pos.shape[0]
    c = jnp.concatenate([cos, cos, jnp.ones((n, rest), F32)], axis=-1)
    s = jnp.concatenate([-sin, sin, jnp.zeros((n, rest), F32)], axis=-1)
    reps = width // HEAD_DIM
    return jnp.tile(c, (1, reps)), jnp.tile(s, (1, reps))


def _swap_perm(width):
    half = ROPE_DIM // 2
    d = np.arange(width)
    dd = d % HEAD_DIM
    return np.where(dd < half, d + half, np.where(dd < ROPE_DIM, d - half, d))


def _kv_kernel(x_ref, g_ref, wkt_ref, w_ref, cost_ref, sint_ref, kt_ref, v_ref, vs_ref, c_ref):
    wk = v_ref.shape[-1]
    sn = _rms(x_ref[...], g_ref[...]).astype(BF16)
    rt = _dot_nt(wkt_ref[...], sn)
    reps = wk // LANES
    cos = jnp.concatenate([cost_ref[...]] * reps, axis=0)
    sin = jnp.concatenate([sint_ref[...]] * reps, axis=0)
    kt_ref[0] = (rt[:wk] * cos + rt[wk:] * sin).astype(kt_ref.dtype)
    r = _dot(sn, w_ref[...])
    v_ref[...] = r[:, :wk].astype(v_ref.dtype)
    vs_ref[...] = r[:, wk:2 * wk].astype(vs_ref.dtype)
    c_ref[...] = r[:, 2 * wk:]


def _kv_call(x2d, g, wkt, w_rest, cos_t, sin_t, b, t, tm):
    n, d = x2d.shape
    wk = KV_HEADS * 2 * HEAD_DIM
    tpt = t // tm
    row = lambda i: (i, 0)
    fix = lambda i: (0, 0)
    tab = lambda i: (0, i % tpt)
    return pl.pallas_call(
        _kv_kernel,
        grid=(n // tm,),
        in_specs=[pl.BlockSpec((tm, d), row), pl.BlockSpec((1, d), fix),
                  pl.BlockSpec(wkt.shape, fix), pl.BlockSpec(w_rest.shape, fix),
                  pl.BlockSpec((LANES, tm), tab), pl.BlockSpec((LANES, tm), tab)],
        out_specs=[pl.BlockSpec((1, wk, tm), lambda i: (i // tpt, 0, i % tpt)),
                   pl.BlockSpec((tm, wk), row), pl.BlockSpec((tm, wk), row), pl.BlockSpec((tm, wk), row)],
        out_shape=[jax.ShapeDtypeStruct((b, wk, t), BF16), jax.ShapeDtypeStruct((n, wk), BF16),
                   jax.ShapeDtypeStruct((n, wk), BF16), jax.ShapeDtypeStruct((n, wk), F32)],
        compiler_params=_cparams(1),
        name="nsa_kv_proj",
    )(x2d, g.reshape(1, d), wkt, w_rest, cos_t, sin_t)


def _cmp_kernel(z_ref, pe_ref, w1_ref, w2kt_ref, w2v_ref, cost_ref, sint_ref, kt_ref, v_ref, *, n_cmp):
    st = CMP_STRIDE
    c = z_ref.shape[1] // st
    hid = w1_ref.shape[-1] // 2
    z = jnp.concatenate([z_ref[0, pl.ds(l, c, stride=st), :] for l in range(st)], axis=1)
    top = _dot((z + pe_ref[0:1, :]).astype(BF16), w1_ref[0])
    bot = _dot((z + pe_ref[1:2, :]).astype(BF16), w1_ref[1])
    hdn = _gelu(top + pltpu.roll(bot, c - 1, 0)).astype(BF16)
    res = [hdn[:, :hid], hdn[:, hid:]]
    kt = (_dot_nt(w2kt_ref[0], res[0]) * cost_ref[...] + _dot_nt(w2kt_ref[1], res[0]) * sint_ref[...])
    col = lax.broadcasted_iota(jnp.int32, kt.shape, 1)
    kt_ref[0, 0] = jnp.where(col < n_cmp, kt, 0.0).astype(kt_ref.dtype)
    v = _dot(res[1], w2v_ref[...])
    row = lax.broadcasted_iota(jnp.int32, v.shape, 0)
    v_ref[0, 0] = jnp.where(row < n_cmp, v, 0.0).astype(v_ref.dtype)


def _cmp_call(ctok, pe, w1, w2kt, w2v, cos_t, sin_t, n_cmp):
    b, t, wid = ctok.shape
    g = wid // LANES
    c = t // CMP_STRIDE
    fix2 = lambda i, j: (0, 0)
    fix3 = lambda i, j: (0, 0, 0)
    return pl.pallas_call(
        functools.partial(_cmp_kernel, n_cmp=n_cmp),
        grid=(b, g),
        in_specs=[pl.BlockSpec((1, t, LANES), lambda i, j: (i, 0, j)),
                  pl.BlockSpec(pe.shape, fix2), pl.BlockSpec(w1.shape, fix3),
                  pl.BlockSpec(w2kt.shape, fix3), pl.BlockSpec(w2v.shape, fix2),
                  pl.BlockSpec(cos_t.shape, fix2), pl.BlockSpec(sin_t.shape, fix2)],
        out_specs=[pl.BlockSpec((1, 1, LANES, c), lambda i, j: (i, j, 0, 0)),
                   pl.BlockSpec((1, 1, c, LANES), lambda i, j: (i, j, 0, 0))],
        out_shape=[jax.ShapeDtypeStruct((b, g, LANES, c), BF16), jax.ShapeDtypeStruct((b, g, c, LANES), BF16)],
        compiler_params=_cparams(2),
        name="nsa_compress",
    )(ctok, pe, w1, w2kt, w2v, cos_t, sin_t)


def _q_kernel(x_ref, g_ref, w_ref, cos_ref, sin_ref, q_ref, gate_ref):
    d = q_ref.shape[-1]
    hn = _rms(x_ref[...], g_ref[...]).astype(BF16)
    r = _dot(hn, w_ref[...])
    reps = d // LANES
    cos = jnp.concatenate([cos_ref[...]] * reps, axis=1)
    sin = jnp.concatenate([sin_ref[...]] * reps, axis=1)
    q_ref[...] = ((r[:, :d] * cos + r[:, d:2 * d] * sin) * (HEAD_DIM ** -0.5 * LOG2E)).astype(q_ref.dtype)
    gate_ref[...] = jax.nn.sigmoid(r[:, 2 * d:])


def _q_call(x2d, g, w_all, cos, sin, t, tm):
    n, d = x2d.shape
    dq = N_HEADS * HEAD_DIM
    tpt = t // tm
    row = lambda i: (i, 0)
    fix = lambda i: (0, 0)
    tab = lambda i: (i % tpt, 0)
    return pl.pallas_call(
        _q_kernel,
        grid=(n // tm,),
        in_specs=[pl.BlockSpec((tm, d), row), pl.BlockSpec((1, d), fix),
                  pl.BlockSpec(w_all.shape, fix),
                  pl.BlockSpec((tm, LANES), tab), pl.BlockSpec((tm, LANES), tab)],
        out_specs=[pl.BlockSpec((tm, dq), row), pl.BlockSpec((tm, LANES), row)],
        out_shape=[jax.ShapeDtypeStruct((n, dq), BF16), jax.ShapeDtypeStruct((n, LANES), F32)],
        compiler_params=_cparams(1),
        name="nsa_q_proj",
    )(x2d, g.reshape(1, d), w_all, cos, sin)


def _attn_kernel_v1(q_ref, kvc_ref, k2_ref, v2_ref, ov_ref, esel_ref, oc_ref, os_ref, ow_ref,
                    selexp_ref, *, n_sb, top_n):
    qb = q_ref.shape[1]
    rep = N_HEADS // KV_HEADS
    hd = HEAD_DIM
    i = pl.program_id(2)
    start = i * qb
    lane = lax.broadcasted_iota(jnp.int32, (qb, LANES), 1)
    lo = lane < hd

    q = q_ref[0].astype(F32)
    zero = jnp.zeros((qb, LANES), F32)
    q_lo, q_hi = [], []
    for p in range(rep * hd // LANES):
        part = q[:, p * LANES:(p + 1) * LANES]
        swapped = pltpu.roll(part, hd, 1)
        q_lo += [jnp.where(lo, part, zero), jnp.where(lo, swapped, zero)]
        q_hi += [jnp.where(lo, zero, swapped), jnp.where(lo, zero, part)]
    q_lo = jnp.concatenate(q_lo, axis=0).astype(BF16)
    q_hi = jnp.concatenate(q_hi, axis=0).astype(BF16)
    rows = rep * qb
    t_rows = start + (lax.broadcasted_iota(jnp.int32, (rows, LANES), 0) & (qb - 1))
    lane_r = lax.broadcasted_iota(jnp.int32, (rows, LANES), 1)

    def unstack(acc, valid_low):
        outs = []
        for p in range(rep // 2):
            a0 = acc[(2 * p) * qb:(2 * p + 1) * qb]
            a1 = acc[(2 * p + 1) * qb:(2 * p + 2) * qb]
            if valid_low:
                outs.append(jnp.where(lo, a0, pltpu.roll(a1, hd, 1)))
            else:
                outs.append(jnp.where(lo, pltpu.roll(a0, hd, 1), a1))
        return jnp.concatenate(outs, axis=1)

    kvc = kvc_ref[0, 0]
    n_c = kvc.shape[0]
    s1 = _dot_nt(q_lo, kvc)
    n_idx = lax.broadcasted_iota(jnp.int32, (rows, n_c), 1)
    t1 = start + (lax.broadcasted_iota(jnp.int32, (rows, n_c), 0) & (qb - 1))
    m1 = n_idx * CMP_STRIDE + (CMP_BLOCK - 1) <= t1
    s1m = jnp.where(m1, s1, NEG)
    e1 = jnp.where(m1, jnp.exp(s1m - jnp.max(s1m, axis=1, keepdims=True)), 0.0)
    l1 = jnp.sum(e1, axis=1, keepdims=True)
    p1 = e1 / jnp.where(l1 > 0.0, l1, 1.0)
    oc_ref[0] = unstack(_dot(p1.astype(BF16), kvc), False)

    psum = p1[0:qb]
    for r in range(1, rep):
        psum = psum + p1[r * qb:(r + 1) * qb]
    imp = jnp.dot(psum, ov_ref[...], preferred_element_type=F32, precision=lax.Precision.HIGHEST)
    t_q = start + lax.broadcasted_iota(jnp.int32, (qb, LANES), 0)
    cur = jnp.right_shift(t_q, int(math.log2(SEL_BLOCK)))
    valid = lane <= cur
    forced = (lane == 0) | (lane == cur) | (lane == cur - 1)
    score = jnp.where(forced, SEL_FORCE, jnp.where(valid, imp, SEL_NEG))
    score = jnp.where(lane < n_sb, score, -jnp.inf)
    lane_f = lane.astype(F32)
    work = score
    sel = jnp.zeros((qb, LANES), F32)
    for _ in range(top_n):
        mx = jnp.max(work, axis=1, keepdims=True)
        first = jnp.min(jnp.where(work == mx, lane_f, float(LANES)), axis=1, keepdims=True)
        hit = lane_f == first
        sel = jnp.where(hit, 1.0, sel)
        work = jnp.where(hit, -jnp.inf, work)
    sel = jnp.where(score > 0.5 * SEL_NEG, sel, 0.0)
    t_len = selexp_ref.shape[1]
    key = lax.broadcasted_iota(jnp.int32, (qb, t_len), 1)
    t_k = start + lax.broadcasted_iota(jnp.int32, (qb, t_len), 0)
    selexp_ref[...] = jnp.where(key <= t_k, _dot(sel.astype(BF16), esel_ref[...]), 0.0)

    def flash_step(qmat, kt, mask, carry):
        m, l, acc = carry
        off = pl.multiple_of(kt * LANES, LANES)
        s = _dot_nt(qmat, k2_ref[0, pl.ds(off, LANES), :])
        sm = jnp.where(mask, s, NEG)
        m_new = jnp.maximum(m, jnp.max(sm, axis=1, keepdims=True))
        alpha = jnp.exp(m - m_new)
        p = jnp.where(mask, jnp.exp(sm - m_new), 0.0)
        l = alpha * l + jnp.sum(p, axis=1, keepdims=True)
        acc = alpha * acc + _dot(p.astype(BF16), v2_ref[0, pl.ds(off, LANES), :])
        return m_new, l, acc

    init = (jnp.full((rows, 1), NEG, F32), jnp.zeros((rows, 1), F32), jnp.zeros((rows, LANES), F32))

    def sel_body(kt, carry):
        off = pl.multiple_of(kt * LANES, LANES)
        mk = selexp_ref[:, pl.ds(off, LANES)]
        mask = jnp.concatenate([mk] * rep, axis=0) > 0.5
        return flash_step(q_lo, kt, mask, carry)

    _, l2, acc2 = lax.fori_loop(0, i + 1, sel_body, init)
    os_ref[0] = unstack(acc2 / l2, True)

    def win_body(kt, carry):
        kpos = kt * LANES + lane_r
        mask = (kpos <= t_rows) & (t_rows - kpos < WINDOW)
        return flash_step(q_hi, kt, mask, carry)

    n_win = WINDOW // LANES
    _, l3, acc3 = lax.fori_loop(jnp.maximum(i - n_win, 0), i + 1, win_body, init)
    ow_ref[0] = unstack(acc3 / l3, False)


def _attn_call_v1(q, kvc, k2, v2, overlap, esel, n_sb):
    b, t, dq = q.shape
    g = KV_HEADS
    gw = dq // g
    c = kvc.shape[2]
    qb = Q_BLOCK
    assert qb == LANES and t % qb == 0
    blk_q = pl.BlockSpec((1, qb, gw), lambda bi, gi, i: (bi, i, gi))
    blk_kv = pl.BlockSpec((1, t, LANES), lambda bi, gi, i: (bi, 0, gi))
    out = jax.ShapeDtypeStruct((b, t, dq), F32)
    return pl.pallas_call(
        functools.partial(_attn_kernel, n_sb=n_sb, top_n=min(SEL_TOPN, n_sb)),
        grid=(b, g, t // qb),
        in_specs=[blk_q,
                  pl.BlockSpec((1, 1, c, LANES), lambda bi, gi, i: (bi, gi, 0, 0)),
                  blk_kv, blk_kv,
                  pl.BlockSpec(overlap.shape, lambda bi, gi, i: (0, 0)),
                  pl.BlockSpec(esel.shape, lambda bi, gi, i: (0, 0))],
        out_specs=[blk_q, blk_q, blk_q],
        out_shape=[out, out, out],
        scratch_shapes=[pltpu.VMEM((qb, t), F32)],
        compiler_params=_cparams(3),
        name="nsa_attention",
    )(q, kvc, k2, v2, overlap, esel)


ATTN_UNROLL = 4
ATTN_Q_ROWS = 256


def _attn_kernel(q_ref, gate_ref, kct_ref, vc_ref, kt_ref, v2_ref, v2s_ref, ovt_ref, esel_ref, eg_ref,
                 o_ref, kcbd_ref, vcbd_ref, kbd_ref, vbd_ref, sc_ref, *, n_sb, top_n):
    qb = q_ref.shape[1]
    rep = N_HEADS // KV_HEADS
    hd = HEAD_DIM
    n_c = kct_ref.shape[-1]
    n_tiles = kt_ref.shape[-1] // LANES
    i = pl.program_id(2)
    start = i * qb

    @pl.when(i == 0)
    def _():
        kcbd_ref[...] = jnp.zeros_like(kcbd_ref)
        vcbd_ref[...] = jnp.zeros_like(vcbd_ref)
        kbd_ref[...] = jnp.zeros_like(kbd_ref)
        vbd_ref[...] = jnp.zeros_like(vbd_ref)
        vc = vc_ref[0, 0]
        for r in range(rep):
            h = hd * (r % 2)
            kcbd_ref[hd * r:hd * (r + 1), n_c * r:n_c * (r + 1)] = kct_ref[0, 0, 0:hd, :]
            vcbd_ref[n_c * r:n_c * (r + 1), hd * r:hd * (r + 1)] = vc[:, h:h + hd]

    qblk = q_ref[0]

    s1 = _dot(qblk, kcbd_ref[...])
    n_idx = lax.broadcasted_iota(jnp.int32, (qb, n_c), 1)
    t1 = start + lax.broadcasted_iota(jnp.int32, (qb, n_c), 0)
    m1 = n_idx * CMP_STRIDE + (CMP_BLOCK - 1) <= t1
    psum = jnp.zeros((qb, n_c), F32)
    p_parts = []
    for r in range(rep):
        sm = jnp.where(m1, s1[:, n_c * r:n_c * (r + 1)], NEG)
        mrow = jnp.max(sm, axis=1, keepdims=True)
        e = jnp.exp2(sm - jnp.where(mrow > 0.5 * NEG, mrow, 0.0))
        l = jnp.sum(e, axis=1, keepdims=True)
        p = e / jnp.where(l > 0.0, l, 1.0)
        psum = psum + p
        p_parts.append(p.astype(BF16))
    o_c = _dot(jnp.concatenate(p_parts, axis=1), vcbd_ref[...])

    p_hi = psum.astype(BF16)
    p_lo = (psum - p_hi.astype(F32)).astype(BF16)
    imp_t = _dot_nt(ovt_ref[...], p_hi) + _dot_nt(ovt_ref[...], p_lo)
    blk = lax.broadcasted_iota(jnp.int32, (n_sb, qb), 0)
    t_l = start + lax.broadcasted_iota(jnp.int32, (n_sb, qb), 1)
    cur = jnp.right_shift(t_l, int(math.log2(SEL_BLOCK)))
    forced = (blk == 0) | (blk == cur) | (blk == cur - 1)
    score = jnp.where(forced, SEL_FORCE, jnp.where(blk <= cur, imp_t[:n_sb], SEL_NEG))
    blk_f = blk.astype(F32)
    work = score
    sel_t = jnp.zeros((n_sb, qb), F32)
    for _ in range(top_n):
        best = jnp.max(work, axis=0, keepdims=True)
        first = jnp.min(jnp.where(work == best, blk_f, float(n_sb)), axis=0, keepdims=True)
        hit = blk_f == first
        sel_t = jnp.where(hit, 1.0, sel_t)
        work = jnp.where(hit, -jnp.inf, work)
    sel_t = jnp.where(score > 0.5 * SEL_NEG, sel_t, 0.0)
    sel = sel_t.T.astype(BF16)

    lane = lax.broadcasted_iota(jnp.int32, (qb, LANES), 1)
    t_q = start + lax.broadcasted_iota(jnp.int32, (qb, LANES), 0)

    def scores(u, off, row0):
        ktile = kt_ref[0, row0:row0 + hd, pl.ds(off, LANES)]
        for r in range(rep):
            kbd_ref[u, hd * r:hd * (r + 1), LANES * r:LANES * (r + 1)] = ktile
        return _dot(qblk, kbd_ref[u])

    def weighted_values(u, off, ps, v_even, v_odd):
        ve = v_even[0, pl.ds(off, LANES), :]
        vo = v_odd[0, pl.ds(off, LANES), :]
        for r in range(rep):
            h = hd * (r % 2)
            src = ve if r % 2 == 0 else vo
            vbd_ref[u, LANES * r:LANES * (r + 1), hd * r:hd * (r + 1)] = src[:, h:h + hd]
        return _dot(jnp.concatenate(ps, axis=1), vbd_ref[u])

    def seg(a, r):
        return a[:, LANES * r:LANES * (r + 1)]

    def row_bcast(parts, op):
        return [jnp.broadcast_to(op(a, axis=1, keepdims=True), (qb, LANES)) for a in parts]

    def normalise(acc, ls):
        lrow = row_bcast(ls, jnp.sum)
        den = jnp.concatenate([jnp.where(lane < hd, lrow[2 * c], lrow[2 * c + 1]) for c in range(rep // 2)],
                              axis=1)
        return acc / den

    tpq = qb // LANES
    n_w = WINDOW // LANES + tpq
    w_first = i * tpq - WINDOW // LANES

    def win_mask(w):
        kpos = (w_first + w) * LANES + lane
        return (kpos >= 0) & (kpos <= t_q) & (t_q - kpos < WINDOW)

    w_off = [pl.multiple_of(jnp.maximum(w_first + w, 0) * LANES, LANES) for w in range(n_w)]
    sm_w = []
    for w in range(n_w):
        s = scores(w, w_off[w], hd)
        mk = win_mask(w)
        sm_w.append([jnp.where(mk, seg(s, r), NEG) for r in range(rep)])
    mx = sm_w[0]
    for w in range(1, n_w):
        mx = [jnp.maximum(mx[r], sm_w[w][r]) for r in range(rep)]
    mrow = row_bcast(mx, jnp.max)
    ls = [jnp.zeros((qb, LANES), F32) for _ in range(rep)]
    acc = jnp.zeros((qb, rep * hd), F32)
    for w in range(n_w):
        ps = [jnp.exp2(sm_w[w][r] - mrow[r]) for r in range(rep)]
        ls = [ls[r] + ps[r] for r in range(rep)]
        acc = acc + weighted_values(w, w_off[w], [p.astype(BF16) for p in ps], v2s_ref, v2_ref)
    o_w = normalise(acc, ls)

    n_it = lax.div((i + 1) * tpq + ATTN_UNROLL - 1, ATTN_UNROLL)

    def tile(p, u):
        kt = p * ATTN_UNROLL + u
        return kt, pl.multiple_of(jnp.minimum(kt, n_tiles - 1) * LANES, LANES)

    def pass1(p, mx):
        mx = list(mx)
        for u in range(ATTN_UNROLL):
            kt, off = tile(p, u)
            picked = _dot(sel, esel_ref[0:n_sb, pl.ds(off, LANES)])
            mk = (picked > 0.5) & (kt * LANES + lane <= t_q)
            s = scores(u, off, 0)
            sm = [jnp.where(mk, seg(s, r), NEG) for r in range(rep)]
            sc_ref[kt] = jnp.concatenate(sm, axis=1)
            mx = [jnp.maximum(mx[r], sm[r]) for r in range(rep)]
        return tuple(mx)

    mx = lax.fori_loop(0, n_it, pass1, tuple(jnp.full((qb, LANES), NEG, F32) for _ in range(rep)))
    mrow = row_bcast(mx, jnp.max)

    def pass2(p, carry):
        ls, acc = carry
        ls = list(ls)
        tiles = [tile(p, u) for u in range(ATTN_UNROLL)]
        ps = [[None] * rep for _ in range(ATTN_UNROLL)]
        for r in range(rep):
            for u in range(ATTN_UNROLL):
                e = jnp.exp2(sc_ref[tiles[u][0], :, LANES * r:LANES * (r + 1)] - mrow[r])
                ls[r] = ls[r] + e
                ps[u][r] = e.astype(BF16)
        for u in range(ATTN_UNROLL):
            acc = acc + weighted_values(u, tiles[u][1], ps[u], v2_ref, v2s_ref)
        return tuple(ls), acc

    ls, acc = lax.fori_loop(0, n_it, pass2, (tuple(jnp.zeros((qb, LANES), F32) for _ in range(rep)),
                                             jnp.zeros((qb, rep * hd), F32)))
    o_s = normalise(acc, ls)

    gw = rep * hd
    gx = jnp.dot(gate_ref[0], eg_ref[0], preferred_element_type=F32, precision=lax.Precision.HIGHEST)
    o_ref[0] = (gx[:, :gw] * o_c + gx[:, gw:2 * gw] * o_s + gx[:, 2 * gw:] * o_w).astype(o_ref.dtype)


def _attn_call(q, gates, kct, vc, kt, v2, v2s, overlap_t, esel, expand, n_sb):
    b, t, dq = q.shape
    g = KV_HEADS
    gw = dq // g
    rep = N_HEADS // KV_HEADS
    c = kct.shape[-1]
    qb = min(ATTN_Q_ROWS, t)
    assert qb % LANES == 0 and t % qb == 0 and rep % 2 == 0 and gw == 2 * LANES
    blk_q = pl.BlockSpec((1, qb, gw), lambda bi, gi, i: (bi, i, gi))
    blk_v = pl.BlockSpec((1, t, LANES), lambda bi, gi, i: (bi, 0, gi))
    fix2 = lambda bi, gi, i: (0, 0)
    n_slots = t // LANES + ATTN_UNROLL
    n_bd = max(ATTN_UNROLL, WINDOW // LANES + qb // LANES)
    return pl.pallas_call(
        functools.partial(_attn_kernel, n_sb=n_sb, top_n=min(SEL_TOPN, n_sb)),
        grid=(b, g, t // qb),
        in_specs=[blk_q,
                  pl.BlockSpec((1, qb, LANES), lambda bi, gi, i: (bi, i, 0)),
                  pl.BlockSpec((1, 1, LANES, c), lambda bi, gi, i: (bi, gi, 0, 0)),
                  pl.BlockSpec((1, 1, c, LANES), lambda bi, gi, i: (bi, gi, 0, 0)),
                  pl.BlockSpec((1, LANES, t), lambda bi, gi, i: (bi, gi, 0)),
                  blk_v, blk_v,
                  pl.BlockSpec(overlap_t.shape, fix2), pl.BlockSpec(esel.shape, fix2),
                  pl.BlockSpec((1,) + expand.shape[1:], lambda bi, gi, i: (gi, 0, 0))],
        out_specs=blk_q,
        out_shape=jax.ShapeDtypeStruct((b, t, dq), BF16),
        scratch_shapes=[pltpu.VMEM((rep * HEAD_DIM, rep * c), BF16),
                        pltpu.VMEM((rep * c, rep * HEAD_DIM), BF16),
                        pltpu.VMEM((n_bd, rep * HEAD_DIM, rep * LANES), BF16),
                        pltpu.VMEM((n_bd, rep * LANES, rep * HEAD_DIM), BF16),
                        pltpu.VMEM((n_slots, qb, rep * LANES), F32)],
        compiler_params=_cparams(3),
        name="nsa_attention",
    )(q, gates, kct, vc, kt, v2, v2s, overlap_t, esel, expand)


def _oproj_kernel(x_ref, o_ref, w_ref, gpost_ref, out_ref):
    out_ref[...] = x_ref[...] + _rms(_dot(o_ref[...], w_ref[...]), gpost_ref[...])


def _oproj_call(x2d, o, w_o, gpost, tm):
    n, d = x2d.shape
    dq = o.shape[-1]
    row = lambda i: (i, 0)
    fix = lambda i: (0, 0)
    return pl.pallas_call(
        _oproj_kernel,
        grid=(n // tm,),
        in_specs=[pl.BlockSpec((tm, d), row), pl.BlockSpec((tm, dq), row),
                  pl.BlockSpec(w_o.shape, fix), pl.BlockSpec((1, d), fix)],
        out_specs=pl.BlockSpec((tm, d), row),
        out_shape=jax.ShapeDtypeStruct((n, d), F32),
        compiler_params=_cparams(1),
        name="nsa_out_proj",
    )(x2d, o, w_o, gpost.reshape(1, d))


def _shared_kv(x, kv_norm_g, w_kv, pe_k, pe_v, k_w1, k_w2, v_w1, v_w2):
    b, t, d = x.shape
    n = b * t
    g, hd = KV_HEADS, HEAD_DIM
    tm = _row_tile(t, 512)
    def cols(ta, tb):
        return np.concatenate([np.concatenate([np.arange(hd) + ta * g * hd + gi * hd,
                                               np.arange(hd) + tb * g * hd + gi * hd]) for gi in range(g)])
    k_cols = cols(2, 4)
    wkt = jnp.concatenate([w_kv[:, k_cols], w_kv[:, k_cols[_swap_perm(k_cols.size)]]], axis=1).T.astype(BF16)
    w_rest = jnp.concatenate([w_kv[:, cols(3, 5)], w_kv[:, cols(5, 3)], w_kv[:, cols(0, 1)]], axis=1).astype(BF16)
    cos, sin = _rope_tables(jnp.arange(t), LANES)
    kt, v2, v2s, ctok = _kv_call(x.reshape(n, d), kv_norm_g, wkt, w_rest, cos.T, sin.T, b, t, tm)
    v2 = v2.reshape(b, t, g * 2 * hd)
    v2s = v2s.reshape(b, t, g * 2 * hd)
    c = t // CMP_STRIDE
    n_cmp = c - CMP_BLOCK // CMP_STRIDE + 1
    assert CMP_BLOCK == 2 * CMP_STRIDE
    st, hid = CMP_STRIDE, k_w1.shape[-1]
    pe = jnp.stack([pe_k.reshape(2, st, hd), pe_v.reshape(2, st, hd)], axis=2).reshape(2, st * 2 * hd).astype(F32)
    zw = jnp.zeros((2, st, hd, hid), k_w1.dtype)
    w1 = jnp.stack([jnp.concatenate([k_w1.reshape(2, st, hd, hid), zw], axis=-1),
                    jnp.concatenate([zw, v_w1.reshape(2, st, hd, hid)], axis=-1)], axis=2)
    w1 = w1.reshape(2, st * 2 * hd, 2 * hid).astype(BF16)
    zpad = jnp.zeros_like(k_w2)
    w2kt = jnp.stack([jnp.concatenate([k_w2, zpad], axis=1).T,
                      jnp.concatenate([k_w2[:, _swap_perm(hd)], zpad], axis=1).T]).astype(BF16)
    w2v = jnp.concatenate([v_w2, v_w2], axis=1).astype(BF16)
    pos_c = jnp.arange(c) * CMP_STRIDE + CMP_BLOCK - 1
    cos_c, sin_c = _rope_tables(pos_c, hd)
    zlane = jnp.zeros((c, LANES - hd), F32)
    cos_c = jnp.concatenate([cos_c, zlane], axis=1).T
    sin_c = jnp.concatenate([sin_c, zlane], axis=1).T
    kct, vc = _cmp_call(ctok.reshape(b, t, g * 2 * hd), pe, w1, w2kt, w2v, cos_c, sin_c, n_cmp)
    return kct, vc, kt, v2, v2s, n_cmp


def _nsa_layer(x, gpre, gpost, w_q, w_o, kct, vc, kt, v2, v2s, n_cmp):
    b, t, d = x.shape
    n = b * t
    hd = HEAD_DIM
    dq = N_HEADS * hd
    tm = _row_tile(t, 512)
    wg = jnp.zeros((d, LANES), w_q.dtype).at[:, :3 * N_HEADS].set(w_q[:, dq:])
    w_all = jnp.concatenate([w_q[:, :dq], w_q[:, :dq][:, _swap_perm(dq)], wg], axis=1).astype(BF16)
    cos, sin = _rope_tables(jnp.arange(t), LANES)
    q, gates = _q_call(x.reshape(n, d), gpre, w_all, cos, sin, t, tm)
    n_sb = t // SEL_BLOCK
    assert n_sb % 8 == 0 and n_sb <= LANES
    c = vc.shape[2]
    ci = np.arange(c)[None, :]
    sj = np.arange(LANES)[:, None]
    overlap_t = ((ci * CMP_STRIDE < (sj + 1) * SEL_BLOCK) & (ci * CMP_STRIDE + CMP_BLOCK > sj * SEL_BLOCK)
                 & (ci < n_cmp) & (sj < n_sb)).astype(np.float32)
    esel = (np.arange(LANES)[:, None] == (np.arange(t)[None, :] // SEL_BLOCK)).astype(np.float32)
    gw = dq // KV_HEADS
    expand = np.zeros((KV_HEADS, LANES, 3 * gw), np.float32)
    for gi in range(KV_HEADS):
        hh = gi * (N_HEADS // KV_HEADS) + np.arange(gw) // hd
        for j in range(3):
            expand[gi, hh * 3 + j, j * gw + np.arange(gw)] = 1.0
    o = _attn_call(q.reshape(b, t, dq), gates.reshape(b, t, LANES), kct, vc, kt, v2, v2s,
                   jnp.asarray(overlap_t, dtype=BF16), jnp.asarray(esel, dtype=BF16), jnp.asarray(expand), n_sb)
    out = _oproj_call(x.reshape(n, d), o.reshape(n, dq), w_o.astype(BF16), gpost, tm)
    return out.reshape(b, t, d)


def kernel(x, a_lam_re, a_lam_im, a_log_dt, a_b_re, a_b_im, a_c_re, a_c_im, a_d, a_w_glu, b_w_q, b_w_o, kv_norm_g, w_kv, cmp_pe_k, cmp_pe_v, cmp_k_w1, cmp_k_w2, cmp_v_w1, cmp_v_w2, mix_pre_g, mix_post_g, ffn_pre_g, ffn_post_g, ffn_w_in, ffn_conv_w, ffn_conv_b, ffn_w_out):
    depth = mix_pre_g.shape[0]
    n_a = depth // 2
    kv = None
    for layer in range(depth):
        if layer < n_a:
            i = layer
            x = _s5_layer(x, mix_pre_g[layer], mix_post_g[layer], a_lam_re[i], a_lam_im[i], a_log_dt[i],
                          a_b_re[i], a_b_im[i], a_c_re[i], a_c_im[i], a_d[i], a_w_glu[i])
        else:
            j = layer - n_a
            x = _nsa_layer(x, mix_pre_g[layer], mix_post_g[layer], b_w_q[j], b_w_o[j], *kv)
        x = _ffn_layer(x, ffn_pre_g[layer], ffn_post_g[layer], ffn_w_in[layer], ffn_conv_w[layer],
                       ffn_conv_b[layer], ffn_w_out[layer])
        if layer == n_a - 1:
            kv = _shared_kv(x, kv_norm_g, w_kv, cmp_pe_k, cmp_pe_v, cmp_k_w1, cmp_k_w2, cmp_v_w1, cmp_v_w2)
    return x
```

```python
import functools
import math

import numpy as np
import jax
import jax.numpy as jnp
from jax import lax
from jax.experimental import pallas as pl
from jax.experimental.pallas import tpu as pltpu

F32 = jnp.float32
BF16 = jnp.bfloat16

S5_GROUP = 16
S5_STATE = 64
N_HEADS = 16
KV_HEADS = 4
HEAD_DIM = 64
CMP_BLOCK = 32
CMP_STRIDE = 16
SEL_BLOCK = 64
SEL_TOPN = 16
WINDOW = 512
Q_BLOCK = 128
ROPE_THETA = 500000.0
ROPE_DIM = HEAD_DIM // 4
CONV_WIDTH = 3
EPS = 1e-6
NEG = -1e30
SEL_FORCE = 1e4
SEL_NEG = -1e4
LOG2E = math.log2(math.e)

LANES = 128
S5_CHUNK = 16
S5_GROUPS_PER_STEP = 8
CONV_HALO = 16
VMEM_LIMIT = 48 * 1024 * 1024


def _cparams(n_axes):
    return pltpu.CompilerParams(dimension_semantics=("arbitrary",) * n_axes,
                                vmem_limit_bytes=VMEM_LIMIT)


def _rms(x, g):
    return x * lax.rsqrt(jnp.mean(x * x, axis=-1, keepdims=True) + EPS) * g


def _gelu(x):
    return jax.nn.gelu(x, approximate=True)


def _dot(a, b):
    return jnp.dot(a, b, preferred_element_type=F32)


def _dot_nt(a, b):
    return lax.dot_general(a, b, (((1,), (1,)), ((), ())), preferred_element_type=F32)


def _row_tile(n, want):
    t = min(n, want)
    assert n % t == 0
    return t


def _s5_tables(lam_re, lam_im, log_dt, b_re, b_im, c_re, c_im, n_chunks):
    hp = lax.Precision.HIGHEST
    L = S5_CHUNK
    G, P = lam_re.shape
    I = b_re.shape[-1]
    dt = jnp.exp(log_dt.astype(F32))[:, None]
    lr, li = lam_re.astype(F32), lam_im.astype(F32)
    mag = jnp.exp(lr * dt)
    ab_re, ab_im = mag * jnp.cos(li * dt), mag * jnp.sin(li * dt)
    nr, ni = ab_re - 1.0, ab_im
    den = lr * lr + li * li
    coef_re = (nr * lr + ni * li) / den
    coef_im = (ni * lr - nr * li) / den
    br, bi = b_re.astype(F32), b_im.astype(F32)
    bb_re = coef_re[..., None] * br - coef_im[..., None] * bi
    bb_im = coef_re[..., None] * bi + coef_im[..., None] * br
    pr = [jnp.ones_like(ab_re)]
    pi = [jnp.zeros_like(ab_re)]
    for _ in range(L):
        r, i = pr[-1], pi[-1]
        pr.append(r * ab_re - i * ab_im)
        pi.append(r * ab_im + i * ab_re)
    pw_re = jnp.stack(pr)
    pw_im = jnp.stack(pi)
    cr, ci = c_re.astype(F32), c_im.astype(F32)
    cl_re = cr[None] * pw_re[:, :, None, :] - ci[None] * pw_im[:, :, None, :]
    cl_im = cr[None] * pw_im[:, :, None, :] + ci[None] * pw_re[:, :, None, :]
    kk = jnp.einsum('kgop,gpi->gkio', jnp.concatenate([cl_re[:L], -cl_im[:L]], axis=-1),
                    jnp.concatenate([bb_re, bb_im], axis=1), precision=hp)
    rev_re = pw_re[L - 1 - np.arange(L)]
    rev_im = pw_im[L - 1 - np.arange(L)]
    pb_re = rev_re[..., None] * bb_re[None] - rev_im[..., None] * bb_im[None]
    pb_im = rev_re[..., None] * bb_im[None] + rev_im[..., None] * bb_re[None]
    pmat = jnp.concatenate([pb_re.transpose(1, 0, 3, 2).reshape(G, L * I, P),
                            pb_im.transpose(1, 0, 3, 2).reshape(G, L * I, P)], axis=-1)
    q_re = cl_re[1:].transpose(1, 3, 0, 2).reshape(G, P, L * I)
    q_im = cl_im[1:].transpose(1, 3, 0, 2).reshape(G, P, L * I)
    qmat = jnp.concatenate([q_re, -q_im], axis=1)
    gl = LANES // I
    nt = G // gl
    own = jnp.asarray(np.arange(LANES)[None, :] // I == np.arange(gl)[:, None], F32)
    eye = jnp.eye(gl, dtype=F32)

    def spread(a):
        return jnp.tile(a, (1,) * (a.ndim - 1) + (gl,))

    bd = spread(kk.reshape(nt, gl, L, I, I)) * own[None, :, None, None, :]
    bd = bd.transpose(0, 2, 1, 3, 4).reshape(nt, L, LANES, LANES).astype(BF16)
    ws = (pmat.reshape(nt, gl, L, I, 2 * P).transpose(0, 2, 1, 3, 4)[:, :, :, :, None, :]
          * eye[None, None, :, None, :, None])
    ws = ws.reshape(nt, L * LANES, gl * 2 * P).astype(BF16)
    qbig = spread(qmat.reshape(nt, gl, 2 * P, L, I)) * own[None, :, None, None, :]
    qbig = qbig.reshape(nt, gl * 2 * P, L * LANES).astype(BF16)
    n_steps = int(math.ceil(math.log2(n_chunks))) if n_chunks > 1 else 0
    mr, mi = pw_re[L], pw_im[L]
    a1, a2 = [], []
    for _ in range(max(n_steps, 1)):
        a1.append(jnp.concatenate([mr, mr], axis=-1).reshape(nt, gl * 2 * P))
        a2.append(jnp.concatenate([-mi, mi], axis=-1).reshape(nt, gl * 2 * P))
        mr, mi = mr * mr - mi * mi, 2.0 * mr * mi
    pad = (-len(a1)) % 8
    a1 = jnp.stack(a1 + [jnp.zeros_like(a1[0])] * pad, axis=1)
    a2 = jnp.stack(a2 + [jnp.zeros_like(a2[0])] * pad, axis=1)
    return bd, ws, qbig, a1, a2, n_steps


def _norm_kernel(x_ref, g_ref, o_ref):
    o_ref[...] = _rms(x_ref[...], g_ref[...]).astype(o_ref.dtype)


def _norm_call(x2d, g, out_dtype, tm):
    n, d = x2d.shape
    return pl.pallas_call(
        _norm_kernel,
        grid=(n // tm,),
        in_specs=[pl.BlockSpec((tm, d), lambda i: (i, 0)),
                  pl.BlockSpec((1, d), lambda i: (0, 0))],
        out_specs=pl.BlockSpec((tm, d), lambda i: (i, 0)),
        out_shape=jax.ShapeDtypeStruct((n, d), out_dtype),
        compiler_params=_cparams(1),
        name="s5_prenorm",
    )(x2d, g.reshape(1, d))


def _s5_scan_kernel(u_ref, bd_ref, ws_ref, q_ref, a1_ref, a2_ref, y_ref, wy_ref, *, n_steps):
    L = S5_CHUNK
    n_chunks = u_ref.shape[1] // L

    @pl.when(pl.program_id(1) == 0)
    def _():
        wy_ref[...] = jnp.zeros_like(wy_ref)
        for s in range(L):
            for r in range(s, L):
                wy_ref[LANES * s:LANES * (s + 1), LANES * r:LANES * (r + 1)] = bd_ref[0, r - s]

    ucat = jnp.concatenate([u_ref[0, pl.ds(s, n_chunks, stride=L), :] for s in range(L)], axis=1).astype(BF16)
    y_intra = _dot(ucat, wy_ref[...])
    ends = _dot(ucat, ws_ref[0])
    row = lax.broadcasted_iota(jnp.int32, (n_chunks, LANES), 0)
    xprev = []
    for g in range(q_ref.shape[1] // LANES):
        x = ends[:, LANES * g:LANES * (g + 1)]
        for j in range(n_steps):
            k = 1 << j
            sh = jnp.where(row >= k, pltpu.roll(x, k, 0), 0.0)
            x = (x + a1_ref[0, j:j + 1, LANES * g:LANES * (g + 1)] * sh
                 + a2_ref[0, j:j + 1, LANES * g:LANES * (g + 1)] * pltpu.roll(sh, S5_STATE, 1))
        xprev.append(jnp.where(row >= 1, pltpu.roll(x, 1, 0), 0.0).astype(BF16))
    y = y_intra + _dot(jnp.concatenate(xprev, axis=1), q_ref[0])
    for s in range(L):
        y_ref[0, pl.ds(s, n_chunks, stride=L), :] = y[:, LANES * s:LANES * (s + 1)]


def _s5_scan_call(u, bd, ws, qbig, a1, a2, n_steps):
    b, t, d = u.shape
    nt = d // LANES
    per_tile = lambda j, i: (j, 0, 0)
    return pl.pallas_call(
        functools.partial(_s5_scan_kernel, n_steps=n_steps),
        grid=(nt, b),
        in_specs=[pl.BlockSpec((1, t, LANES), lambda j, i: (i, 0, j)),
                  pl.BlockSpec((1,) + bd.shape[1:], lambda j, i: (j, 0, 0, 0)),
                  pl.BlockSpec((1,) + ws.shape[1:], per_tile),
                  pl.BlockSpec((1,) + qbig.shape[1:], per_tile),
                  pl.BlockSpec((1,) + a1.shape[1:], per_tile),
                  pl.BlockSpec((1,) + a2.shape[1:], per_tile)],
        out_specs=pl.BlockSpec((1, t, LANES), lambda j, i: (i, 0, j)),
        out_shape=jax.ShapeDtypeStruct((b, t, d), F32),
        scratch_shapes=[pltpu.VMEM((S5_CHUNK * LANES, S5_CHUNK * LANES), BF16)],
        compiler_params=_cparams(2),
        name="s5_scan",
    )(u, bd, ws, qbig, a1, a2)


def _s5_out_kernel(x_ref, y_ref, gpre_ref, d_ref, w_ref, gpost_ref, o_ref):
    x = x_ref[...]
    u = _rms(x, gpre_ref[...])
    z = _gelu(y_ref[...] + u * d_ref[...]).astype(BF16)
    ag = _dot(z, w_ref[...])
    d = x.shape[-1]
    m = ag[:, :d] * jax.nn.sigmoid(ag[:, d:])
    o_ref[...] = x + _rms(m, gpost_ref[...])


def _s5_out_call(x2d, y2d, gpre, dskip, wglu, gpost, tm):
    n, d = x2d.shape
    row = lambda i: (i, 0)
    fix = lambda i: (0, 0)
    return pl.pallas_call(
        _s5_out_kernel,
        grid=(n // tm,),
        in_specs=[pl.BlockSpec((tm, d), row), pl.BlockSpec((tm, d), row),
                  pl.BlockSpec((1, d), fix), pl.BlockSpec((1, d), fix),
                  pl.BlockSpec(wglu.shape, fix), pl.BlockSpec((1, d), fix)],
        out_specs=pl.BlockSpec((tm, d), row),
        out_shape=jax.ShapeDtypeStruct((n, d), F32),
        compiler_params=_cparams(1),
        name="s5_glu_out",
    )(x2d, y2d, gpre.reshape(1, d), dskip.reshape(1, d), wglu, gpost.reshape(1, d))


def _s5_layer(x, gpre, gpost, lam_re, lam_im, log_dt, b_re, b_im, c_re, c_im, d_skip, w_glu):
    b, t, d = x.shape
    n = b * t
    tm = _row_tile(n, 512)
    assert t % (8 * S5_CHUNK) == 0 and d % LANES == 0 and LANES % S5_GROUP == 0
    bd, ws, qbig, a1, a2, n_steps = _s5_tables(lam_re, lam_im, log_dt, b_re, b_im, c_re, c_im, t // S5_CHUNK)
    x2d = x.reshape(n, d)
    u = _norm_call(x2d, gpre, F32, tm)
    y = _s5_scan_call(u.reshape(b, t, d), bd, ws, qbig, a1, a2, n_steps)
    out = _s5_out_call(x2d, y.reshape(n, d), gpre, d_skip, w_glu.astype(BF16), gpost, tm)
    return out.reshape(b, t, d)


def _ffn_kernel(xprev_ref, x_ref, gpre_ref, wg_ref, wv_ref, cw_ref, cb_ref, wo_ref, gpost_ref,
                o_ref, xn_ref, acc_ref, *, seq_tiles):
    i = pl.program_id(0)
    j = pl.program_id(1)
    h = CONV_HALO

    @pl.when(j == 0)
    def _():
        g = gpre_ref[...]
        xn_ref[h:, :] = _rms(x_ref[...], g).astype(BF16)
        keep = jnp.where(i % seq_tiles == 0, 0.0, 1.0)
        xn_ref[:h, :] = (_rms(xprev_ref[...], g) * keep).astype(BF16)
        acc_ref[...] = jnp.zeros_like(acc_ref)

    xn = xn_ref[...]
    gate = _dot(xn, wg_ref[...])
    val = _dot(xn[h:], wv_ref[...])
    cw = cw_ref[...]
    conv = (cw[0:1] * pltpu.roll(gate, 2, 0) + cw[1:2] * pltpu.roll(gate, 1, 0)
            + cw[2:3] * gate + cb_ref[...])
    act = _gelu(conv[h:]) * val
    acc_ref[...] += _dot(act.astype(BF16), wo_ref[...])

    @pl.when(j == pl.num_programs(1) - 1)
    def _():
        o_ref[...] = x_ref[...] + _rms(acc_ref[...], gpost_ref[...])


def _ffn_layer(x, gpre, gpost, w_in, conv_w, conv_b, w_out):
    b, t, d = x.shape
    f = w_out.shape[0]
    n = b * t
    tm = _row_tile(t, 1024)
    tf = 256 if f % 256 == 0 else LANES
    assert f % tf == 0 and tm % CONV_HALO == 0
    nf = f // tf
    h = CONV_HALO
    x2d = x.reshape(n, d)
    cw = jnp.zeros((8, f), F32).at[:CONV_WIDTH].set(conv_w.astype(F32))
    w_in16 = w_in.astype(BF16)
    out = pl.pallas_call(
        functools.partial(_ffn_kernel, seq_tiles=t // tm),
        grid=(n // tm, nf),
        in_specs=[pl.BlockSpec((h, d), lambda i, j: (jnp.maximum(i * (tm // h) - 1, 0), 0)),
                  pl.BlockSpec((tm, d), lambda i, j: (i, 0)),
                  pl.BlockSpec((1, d), lambda i, j: (0, 0)),
                  pl.BlockSpec((d, tf), lambda i, j: (0, j)),
                  pl.BlockSpec((d, tf), lambda i, j: (0, nf + j)),
                  pl.BlockSpec((8, tf), lambda i, j: (0, j)),
                  pl.BlockSpec((1, tf), lambda i, j: (0, j)),
                  pl.BlockSpec((tf, d), lambda i, j: (j, 0)),
                  pl.BlockSpec((1, d), lambda i, j: (0, 0))],
        out_specs=pl.BlockSpec((tm, d), lambda i, j: (i, 0)),
        out_shape=jax.ShapeDtypeStruct((n, d), F32),
        scratch_shapes=[pltpu.VMEM((tm + h, d), BF16), pltpu.VMEM((tm, d), F32)],
        compiler_params=_cparams(2),
        name="conv_ffn",
    )(x2d, x2d, gpre.reshape(1, d), w_in16, w_in16, cw, conv_b.reshape(1, f).astype(F32),
      w_out.astype(BF16), gpost.reshape(1, d))
    return out.reshape(b, t, d)


def _rope_tables(pos, width):
    half = ROPE_DIM // 2
    inv = ROPE_THETA ** (-jnp.arange(half, dtype=F32) / half)
    ang = pos.astype(F32)[:, None] * inv[None, :]
    cos, sin = jnp.cos(ang), jnp.sin(ang)
    rest = HEAD_DIM - ROPE_DIM
    n = pos.shape[0]
    c = jnp.concatenate([cos, cos, jnp.ones((n, rest), F32)], axis=-1)
    s = jnp.concatenate([-sin, sin, jnp.zeros((n, rest), F32)], axis=-1)
    reps = width // HEAD_DIM
    return jnp.tile(c, (1, reps)), jnp.tile(s, (1, reps))


def _swap_perm(width):
    half = ROPE_DIM // 2
    d = np.arange(width)
    dd = d % HEAD_DIM
    return np.where(dd < half, d + half, np.where(dd < ROPE_DIM, d - half, d))


def _kv_kernel(x_ref, g_ref, wkt_ref, w_ref, cost_ref, sint_ref, kt_ref, v_ref, vs_ref, c_ref):
    wk = v_ref.shape[-1]
    sn = _rms(x_ref[...], g_ref[...]).astype(BF16)
    rt = _dot_nt(wkt_ref[...], sn)
    reps = wk // LANES
    cos = jnp.concatenate([cost_ref[...]] * reps, axis=0)
    sin = jnp.concatenate([sint_ref[...]] * reps, axis=0)
    kt_ref[0] = (rt[:wk] * cos + rt[wk:] * sin).astype(kt_ref.dtype)
    r = _dot(sn, w_ref[...])
    v_ref[...] = r[:, :wk].astype(v_ref.dtype)
    vs_ref[...] = r[:, wk:2 * wk].astype(vs_ref.dtype)
    c_ref[...] = r[:, 2 * wk:]


def _kv_call(x2d, g, wkt, w_rest, cos_t, sin_t, b, t, tm):
    n, d = x2d.shape
    wk = KV_HEADS * 2 * HEAD_DIM
    tpt = t // tm
    row = lambda i: (i, 0)
    fix = lambda i: (0, 0)
    tab = lambda i: (0, i % tpt)
    return pl.pallas_call(
        _kv_kernel,
        grid=(n // tm,),
        in_specs=[pl.BlockSpec((tm, d), row), pl.BlockSpec((1, d), fix),
                  pl.BlockSpec(wkt.shape, fix), pl.BlockSpec(w_rest.shape, fix),
                  pl.BlockSpec((LANES, tm), tab), pl.BlockSpec((LANES, tm), tab)],
        out_specs=[pl.BlockSpec((1, wk, tm), lambda i: (i // tpt, 0, i % tpt)),
                   pl.BlockSpec((tm, wk), row), pl.BlockSpec((tm, wk), row), pl.BlockSpec((tm, wk), row)],
        out_shape=[jax.ShapeDtypeStruct((b, wk, t), BF16), jax.ShapeDtypeStruct((n, wk), BF16),
                   jax.ShapeDtypeStruct((n, wk), BF16), jax.ShapeDtypeStruct((n, wk), F32)],
        compiler_params=_cparams(1),
        name="nsa_kv_proj",
    )(x2d, g.reshape(1, d), wkt, w_rest, cos_t, sin_t)


def _cmp_kernel(z_ref, pe_ref, w1_ref, w2kt_ref, w2v_ref, cost_ref, sint_ref, kt_ref, v_ref, *, n_cmp):
    st = CMP_STRIDE
    c = z_ref.shape[1] // st
    hid = w1_ref.shape[-1] // 2
    z = jnp.concatenate([z_ref[0, pl.ds(l, c, stride=st), :] for l in range(st)], axis=1)
    top = _dot((z + pe_ref[0:1, :]).astype(BF16), w1_ref[0])
    bot = _dot((z + pe_ref[1:2, :]).astype(BF16), w1_ref[1])
    hdn = _gelu(top + pltpu.roll(bot, c - 1, 0)).astype(BF16)
    res = [hdn[:, :hid], hdn[:, hid:]]
    kt = (_dot_nt(w2kt_ref[0], res[0]) * cost_ref[...] + _dot_nt(w2kt_ref[1], res[0]) * sint_ref[...])
    col = lax.broadcasted_iota(jnp.int32, kt.shape, 1)
    kt_ref[0, 0] = jnp.where(col < n_cmp, kt, 0.0).astype(kt_ref.dtype)
    v = _dot(res[1], w2v_ref[...])
    row = lax.broadcasted_iota(jnp.int32, v.shape, 0)
    v_ref[0, 0] = jnp.where(row < n_cmp, v, 0.0).astype(v_ref.dtype)


def _cmp_call(ctok, pe, w1, w2kt, w2v, cos_t, sin_t, n_cmp):
    b, t, wid = ctok.shape
    g = wid // LANES
    c = t // CMP_STRIDE
    fix2 = lambda i, j: (0, 0)
    fix3 = lambda i, j: (0, 0, 0)
    return pl.pallas_call(
        functools.partial(_cmp_kernel, n_cmp=n_cmp),
        grid=(b, g),
        in_specs=[pl.BlockSpec((1, t, LANES), lambda i, j: (i, 0, j)),
                  pl.BlockSpec(pe.shape, fix2), pl.BlockSpec(w1.shape, fix3),
                  pl.BlockSpec(w2kt.shape, fix3), pl.BlockSpec(w2v.shape, fix2),
                  pl.BlockSpec(cos_t.shape, fix2), pl.BlockSpec(sin_t.shape, fix2)],
        out_specs=[pl.BlockSpec((1, 1, LANES, c), lambda i, j: (i, j, 0, 0)),
                   pl.BlockSpec((1, 1, c, LANES), lambda i, j: (i, j, 0, 0))],
        out_shape=[jax.ShapeDtypeStruct((b, g, LANES, c), BF16), jax.ShapeDtypeStruct((b, g, c, LANES), BF16)],
        compiler_params=_cparams(2),
        name="nsa_compress",
    )(ctok, pe, w1, w2kt, w2v, cos_t, sin_t)


def _q_kernel(x_ref, g_ref, w_ref, cos_ref, sin_ref, q_ref, gate_ref):
    d = q_ref.shape[-1]
    hn = _rms(x_ref[...], g_ref[...]).astype(BF16)
    r = _dot(hn, w_ref[...])
    reps = d // LANES
    cos = jnp.concatenate([cos_ref[...]] * reps, axis=1)
    sin = jnp.concatenate([sin_ref[...]] * reps, axis=1)
    q_ref[...] = ((r[:, :d] * cos + r[:, d:2 * d] * sin) * (HEAD_DIM ** -0.5 * LOG2E)).astype(q_ref.dtype)
    gate_ref[...] = jax.nn.sigmoid(r[:, 2 * d:])


def _q_call(x2d, g, w_all, cos, sin, t, tm):
    n, d = x2d.shape
    dq = N_HEADS * HEAD_DIM
    tpt = t // tm
    row = lambda i: (i, 0)
    fix = lambda i: (0, 0)
    tab = lambda i: (i % tpt, 0)
    return pl.pallas_call(
        _q_kernel,
        grid=(n // tm,),
        in_specs=[pl.BlockSpec((tm, d), row), pl.BlockSpec((1, d), fix),
                  pl.BlockSpec(w_all.shape, fix),
                  pl.BlockSpec((tm, LANES), tab), pl.BlockSpec((tm, LANES), tab)],
        out_specs=[pl.BlockSpec((tm, dq), row), pl.BlockSpec((tm, LANES), row)],
        out_shape=[jax.ShapeDtypeStruct((n, dq), BF16), jax.ShapeDtypeStruct((n, LANES), F32)],
        compiler_params=_cparams(1),
        name="nsa_q_proj",
    )(x2d, g.reshape(1, d), w_all, cos, sin)


def _attn_kernel_v1(q_ref, kvc_ref, k2_ref, v2_ref, ov_ref, esel_ref, oc_ref, os_ref, ow_ref,
                    selexp_ref, *, n_sb, top_n):
    qb = q_ref.shape[1]
    rep = N_HEADS // KV_HEADS
    hd = HEAD_DIM
    i = pl.program_id(2)
    start = i * qb
    lane = lax.broadcasted_iota(jnp.int32, (qb, LANES), 1)
    lo = lane < hd

    q = q_ref[0].astype(F32)
    zero = jnp.zeros((qb, LANES), F32)
    q_lo, q_hi = [], []
    for p in range(rep * hd // LANES):
        part = q[:, p * LANES:(p + 1) * LANES]
        swapped = pltpu.roll(part, hd, 1)
        q_lo += [jnp.where(lo, part, zero), jnp.where(lo, swapped, zero)]
        q_hi += [jnp.where(lo, zero, swapped), jnp.where(lo, zero, part)]
    q_lo = jnp.concatenate(q_lo, axis=0).astype(BF16)
    q_hi = jnp.concatenate(q_hi, axis=0).astype(BF16)
    rows = rep * qb
    t_rows = start + (lax.broadcasted_iota(jnp.int32, (rows, LANES), 0) & (qb - 1))
    lane_r = lax.broadcasted_iota(jnp.int32, (rows, LANES), 1)

    def unstack(acc, valid_low):
        outs = []
        for p in range(rep // 2):
            a0 = acc[(2 * p) * qb:(2 * p + 1) * qb]
            a1 = acc[(2 * p + 1) * qb:(2 * p + 2) * qb]
            if valid_low:
                outs.append(jnp.where(lo, a0, pltpu.roll(a1, hd, 1)))
            else:
                outs.append(jnp.where(lo, pltpu.roll(a0, hd, 1), a1))
        return jnp.concatenate(outs, axis=1)

    kvc = kvc_ref[0, 0]
    n_c = kvc.shape[0]
    s1 = _dot_nt(q_lo, kvc)
    n_idx = lax.broadcasted_iota(jnp.int32, (rows, n_c), 1)
    t1 = start + (lax.broadcasted_iota(jnp.int32, (rows, n_c), 0) & (qb - 1))
    m1 = n_idx * CMP_STRIDE + (CMP_BLOCK - 1) <= t1
    s1m = jnp.where(m1, s1, NEG)
    e1 = jnp.where(m1, jnp.exp(s1m - jnp.max(s1m, axis=1, keepdims=True)), 0.0)
    l1 = jnp.sum(e1, axis=1, keepdims=True)
    p1 = e1 / jnp.where(l1 > 0.0, l1, 1.0)
    oc_ref[0] = unstack(_dot(p1.astype(BF16), kvc), False)

    psum = p1[0:qb]
    for r in range(1, rep):
        psum = psum + p1[r * qb:(r + 1) * qb]
    imp = jnp.dot(psum, ov_ref[...], preferred_element_type=F32, precision=lax.Precision.HIGHEST)
    t_q = start + lax.broadcasted_iota(jnp.int32, (qb, LANES), 0)
    cur = jnp.right_shift(t_q, int(math.log2(SEL_BLOCK)))
    valid = lane <= cur
    forced = (lane == 0) | (lane == cur) | (lane == cur - 1)
    score = jnp.where(forced, SEL_FORCE, jnp.where(valid, imp, SEL_NEG))
    score = jnp.where(lane < n_sb, score, -jnp.inf)
    lane_f = lane.astype(F32)
    work = score
    sel = jnp.zeros((qb, LANES), F32)
    for _ in range(top_n):
        mx = jnp.max(work, axis=1, keepdims=True)
        first = jnp.min(jnp.where(work == mx, lane_f, float(LANES)), axis=1, keepdims=True)
        hit = lane_f == first
        sel = jnp.where(hit, 1.0, sel)
        work = jnp.where(hit, -jnp.inf, work)
    sel = jnp.where(score > 0.5 * SEL_NEG, sel, 0.0)
    t_len = selexp_ref.shape[1]
    key = lax.broadcasted_iota(jnp.int32, (qb, t_len), 1)
    t_k = start + lax.broadcasted_iota(jnp.int32, (qb, t_len), 0)
    selexp_ref[...] = jnp.where(key <= t_k, _dot(sel.astype(BF16), esel_ref[...]), 0.0)

    def flash_step(qmat, kt, mask, carry):
        m, l, acc = carry
        off = pl.multiple_of(kt * LANES, LANES)
        s = _dot_nt(qmat, k2_ref[0, pl.ds(off, LANES), :])
        sm = jnp.where(mask, s, NEG)
        m_new = jnp.maximum(m, jnp.max(sm, axis=1, keepdims=True))
        alpha = jnp.exp(m - m_new)
        p = jnp.where(mask, jnp.exp(sm - m_new), 0.0)
        l = alpha * l + jnp.sum(p, axis=1, keepdims=True)
        acc = alpha * acc + _dot(p.astype(BF16), v2_ref[0, pl.ds(off, LANES), :])
        return m_new, l, acc

    init = (jnp.full((rows, 1), NEG, F32), jnp.zeros((rows, 1), F32), jnp.zeros((rows, LANES), F32))

    def sel_body(kt, carry):
        off = pl.multiple_of(kt * LANES, LANES)
        mk = selexp_ref[:, pl.ds(off, LANES)]
        mask = jnp.concatenate([mk] * rep, axis=0) > 0.5
        return flash_step(q_lo, kt, mask, carry)

    _, l2, acc2 = lax.fori_loop(0, i + 1, sel_body, init)
    os_ref[0] = unstack(acc2 / l2, True)

    def win_body(kt, carry):
        kpos = kt * LANES + lane_r
        mask = (kpos <= t_rows) & (t_rows - kpos < WINDOW)
        return flash_step(q_hi, kt, mask, carry)

    n_win = WINDOW // LANES
    _, l3, acc3 = lax.fori_loop(jnp.maximum(i - n_win, 0), i + 1, win_body, init)
    ow_ref[0] = unstack(acc3 / l3, False)


def _attn_call_v1(q, kvc, k2, v2, overlap, esel, n_sb):
    b, t, dq = q.shape
    g = KV_HEADS
    gw = dq // g
    c = kvc.shape[2]
    qb = Q_BLOCK
    assert qb == LANES and t % qb == 0
    blk_q = pl.BlockSpec((1, qb, gw), lambda bi, gi, i: (bi, i, gi))
    blk_kv = pl.BlockSpec((1, t, LANES), lambda bi, gi, i: (bi, 0, gi))
    out = jax.ShapeDtypeStruct((b, t, dq), F32)
    return pl.pallas_call(
        functools.partial(_attn_kernel, n_sb=n_sb, top_n=min(SEL_TOPN, n_sb)),
        grid=(b, g, t // qb),
        in_specs=[blk_q,
                  pl.BlockSpec((1, 1, c, LANES), lambda bi, gi, i: (bi, gi, 0, 0)),
                  blk_kv, blk_kv,
                  pl.BlockSpec(overlap.shape, lambda bi, gi, i: (0, 0)),
                  pl.BlockSpec(esel.shape, lambda bi, gi, i: (0, 0))],
        out_specs=[blk_q, blk_q, blk_q],
        out_shape=[out, out, out],
        scratch_shapes=[pltpu.VMEM((qb, t), F32)],
        compiler_params=_cparams(3),
        name="nsa_attention",
    )(q, kvc, k2, v2, overlap, esel)


ATTN_UNROLL = 4
ATTN_Q_ROWS = 256


def _attn_kernel(q_ref, gate_ref, kct_ref, vc_ref, kt_ref, v2_ref, v2s_ref, ovt_ref, esel_ref, eg_ref,
                 o_ref, kcbd_ref, vcbd_ref, kbd_ref, vbd_ref, sc_ref, *, n_sb, top_n):
    qb = q_ref.shape[1]
    rep = N_HEADS // KV_HEADS
    hd = HEAD_DIM
    n_c = kct_ref.shape[-1]
    n_tiles = kt_ref.shape[-1] // LANES
    i = pl.program_id(2)
    start = i * qb

    @pl.when(i == 0)
    def _():
        kcbd_ref[...] = jnp.zeros_like(kcbd_ref)
        vcbd_ref[...] = jnp.zeros_like(vcbd_ref)
        kbd_ref[...] = jnp.zeros_like(kbd_ref)
        vbd_ref[...] = jnp.zeros_like(vbd_ref)
        vc = vc_ref[0, 0]
        for r in range(rep):
            h = hd * (r % 2)
            kcbd_ref[hd * r:hd * (r + 1), n_c * r:n_c * (r + 1)] = kct_ref[0, 0, 0:hd, :]
            vcbd_ref[n_c * r:n_c * (r + 1), hd * r:hd * (r + 1)] = vc[:, h:h + hd]

    qblk = q_ref[0]

    s1 = _dot(qblk, kcbd_ref[...])
    n_idx = lax.broadcasted_iota(jnp.int32, (qb, n_c), 1)
    t1 = start + lax.broadcasted_iota(jnp.int32, (qb, n_c), 0)
    m1 = n_idx * CMP_STRIDE + (CMP_BLOCK - 1) <= t1
    psum = jnp.zeros((qb, n_c), F32)
    p_parts = []
    for r in range(rep):
        sm = jnp.where(m1, s1[:, n_c * r:n_c * (r + 1)], NEG)
        mrow = jnp.max(sm, axis=1, keepdims=True)
        e = jnp.exp2(sm - jnp.where(mrow > 0.5 * NEG, mrow, 0.0))
        l = jnp.sum(e, axis=1, keepdims=True)
        p = e / jnp.where(l > 0.0, l, 1.0)
        psum = psum + p
        p_parts.append(p.astype(BF16))
    o_c = _dot(jnp.concatenate(p_parts, axis=1), vcbd_ref[...])

    p_hi = psum.astype(BF16)
    p_lo = (psum - p_hi.astype(F32)).astype(BF16)
    imp_t = _dot_nt(ovt_ref[...], p_hi) + _dot_nt(ovt_ref[...], p_lo)
    blk = lax.broadcasted_iota(jnp.int32, (n_sb, qb), 0)
    t_l = start + lax.broadcasted_iota(jnp.int32, (n_sb, qb), 1)
    cur = jnp.right_shift(t_l, int(math.log2(SEL_BLOCK)))
    forced = (blk == 0) | (blk == cur) | (blk == cur - 1)
    score = jnp.where(forced, SEL_FORCE, jnp.where(blk <= cur, imp_t[:n_sb], SEL_NEG))
    blk_f = blk.astype(F32)
    work = score
    sel_t = jnp.zeros((n_sb, qb), F32)
    for _ in range(top_n):
        best = jnp.max(work, axis=0, keepdims=True)
        first = jnp.min(jnp.where(work == best, blk_f, float(n_sb)), axis=0, keepdims=True)
        hit = blk_f == first
        sel_t = jnp.where(hit, 1.0, sel_t)
        work = jnp.where(hit, -jnp.inf, work)
    sel_t = jnp.where(score > 0.5 * SEL_NEG, sel_t, 0.0)
    sel = sel_t.T.astype(BF16)

    lane = lax.broadcasted_iota(jnp.int32, (qb, LANES), 1)
    t_q = start + lax.broadcasted_iota(jnp.int32, (qb, LANES), 0)

    def scores(u, off, row0):
        ktile = kt_ref[0, row0:row0 + hd, pl.ds(off, LANES)]
        for r in range(rep):
            kbd_ref[u, hd * r:hd * (r + 1), LANES * r:LANES * (r + 1)] = ktile
        return _dot(qblk, kbd_ref[u])

    def weighted_values(u, off, ps, v_even, v_odd):
        ve = v_even[0, pl.ds(off, LANES), :]
        vo = v_odd[0, pl.ds(off, LANES), :]
        for r in range(rep):
            h = hd * (r % 2)
            src = ve if r % 2 == 0 else vo
            vbd_ref[u, LANES * r:LANES * (r + 1), hd * r:hd * (r + 1)] = src[:, h:h + hd]
        return _dot(jnp.concatenate(ps, axis=1), vbd_ref[u])

    def seg(a, r):
        return a[:, LANES * r:LANES * (r + 1)]

    def row_bcast(parts, op):
        return [jnp.broadcast_to(op(a, axis=1, keepdims=True), (qb, LANES)) for a in parts]

    def normalise(acc, ls):
        lrow = row_bcast(ls, jnp.sum)
        den = jnp.concatenate([jnp.where(lane < hd, lrow[2 * c], lrow[2 * c + 1]) for c in range(rep // 2)],
                              axis=1)
        return acc / den

    tpq = qb // LANES
    n_w = WINDOW // LANES + tpq
    w_first = i * tpq - WINDOW // LANES

    def win_mask(w):
        kpos = (w_first + w) * LANES + lane
        return (kpos >= 0) & (kpos <= t_q) & (t_q - kpos < WINDOW)

    w_off = [pl.multiple_of(jnp.maximum(w_first + w, 0) * LANES, LANES) for w in range(n_w)]
    sm_w = []
    for w in range(n_w):
        s = scores(w, w_off[w], hd)
        mk = win_mask(w)
        sm_w.append([jnp.where(mk, seg(s, r), NEG) for r in range(rep)])
    mx = sm_w[0]
    for w in range(1, n_w):
        mx = [jnp.maximum(mx[r], sm_w[w][r]) for r in range(rep)]
    mrow = row_bcast(mx, jnp.max)
    ls = [jnp.zeros((qb, LANES), F32) for _ in range(rep)]
    acc = jnp.zeros((qb, rep * hd), F32)
    for w in range(n_w):
        ps = [jnp.exp2(sm_w[w][r] - mrow[r]) for r in range(rep)]
        ls = [ls[r] + ps[r] for r in range(rep)]
        acc = acc + weighted_values(w, w_off[w], [p.astype(BF16) for p in ps], v2s_ref, v2_ref)
    o_w = normalise(acc, ls)

    n_it = lax.div((i + 1) * tpq + ATTN_UNROLL - 1, ATTN_UNROLL)

    def tile(p, u):
        kt = p * ATTN_UNROLL + u
        return kt, pl.multiple_of(jnp.minimum(kt, n_tiles - 1) * LANES, LANES)

    def pass1(p, mx):
        mx = list(mx)
        for u in range(ATTN_UNROLL):
            kt, off = tile(p, u)
            picked = _dot(sel, esel_ref[0:n_sb, pl.ds(off, LANES)])
            mk = (picked > 0.5) & (kt * LANES + lane <= t_q)
            s = scores(u, off, 0)
            sm = [jnp.where(mk, seg(s, r), NEG) for r in range(rep)]
            sc_ref[kt] = jnp.concatenate(sm, axis=1)
            mx = [jnp.maximum(mx[r], sm[r]) for r in range(rep)]
        return tuple(mx)

    mx = lax.fori_loop(0, n_it, pass1, tuple(jnp.full((qb, LANES), NEG, F32) for _ in range(rep)))
    mrow = row_bcast(mx, jnp.max)

    def pass2(p, carry):
        ls, acc = carry
        ls = list(ls)
        tiles = [tile(p, u) for u in range(ATTN_UNROLL)]
        ps = [[None] * rep for _ in range(ATTN_UNROLL)]
        for r in range(rep):
            for u in range(ATTN_UNROLL):
                e = jnp.exp2(sc_ref[tiles[u][0], :, LANES * r:LANES * (r + 1)] - mrow[r])
                ls[r] = ls[r] + e
                ps[u][r] = e.astype(BF16)
        for u in range(ATTN_UNROLL):
            acc = acc + weighted_values(u, tiles[u][1], ps[u], v2_ref, v2s_ref)
        return tuple(ls), acc

    ls, acc = lax.fori_loop(0, n_it, pass2, (tuple(jnp.zeros((qb, LANES), F32) for _ in range(rep)),
                                             jnp.zeros((qb, rep * hd), F32)))
    o_s = normalise(acc, ls)

    gw = rep * hd
    g_hi = gate_ref[0].astype(BF16)
    g_lo = (gate_ref[0] - g_hi.astype(F32)).astype(BF16)
    gx = _dot(g_hi, eg_ref[0]) + _dot(g_lo, eg_ref[0])
    o_ref[0] = (gx[:, :gw] * o_c + gx[:, gw:2 * gw] * o_s + gx[:, 2 * gw:] * o_w).astype(o_ref.dtype)


def _attn_call(q, gates, kct, vc, kt, v2, v2s, overlap_t, esel, expand, n_sb):
    b, t, dq = q.shape
    g = KV_HEADS
    gw = dq // g
    rep = N_HEADS // KV_HEADS
    c = kct.shape[-1]
    qb = min(ATTN_Q_ROWS, t)
    assert qb % LANES == 0 and t % qb == 0 and rep % 2 == 0 and gw == 2 * LANES
    blk_q = pl.BlockSpec((1, qb, gw), lambda bi, gi, i: (bi, i, gi))
    blk_v = pl.BlockSpec((1, t, LANES), lambda bi, gi, i: (bi, 0, gi))
    fix2 = lambda bi, gi, i: (0, 0)
    n_slots = t // LANES + ATTN_UNROLL
    n_bd = max(ATTN_UNROLL, WINDOW // LANES + qb // LANES)
    return pl.pallas_call(
        functools.partial(_attn_kernel, n_sb=n_sb, top_n=min(SEL_TOPN, n_sb)),
        grid=(b, g, t // qb),
        in_specs=[blk_q,
                  pl.BlockSpec((1, qb, LANES), lambda bi, gi, i: (bi, i, 0)),
                  pl.BlockSpec((1, 1, LANES, c), lambda bi, gi, i: (bi, gi, 0, 0)),
                  pl.BlockSpec((1, 1, c, LANES), lambda bi, gi, i: (bi, gi, 0, 0)),
                  pl.BlockSpec((1, LANES, t), lambda bi, gi, i: (bi, gi, 0)),
                  blk_v, blk_v,
                  pl.BlockSpec(overlap_t.shape, fix2), pl.BlockSpec(esel.shape, fix2),
                  pl.BlockSpec((1,) + expand.shape[1:], lambda bi, gi, i: (gi, 0, 0))],
        out_specs=blk_q,
        out_shape=jax.ShapeDtypeStruct((b, t, dq), BF16),
        scratch_shapes=[pltpu.VMEM((rep * HEAD_DIM, rep * c), BF16),
                        pltpu.VMEM((rep * c, rep * HEAD_DIM), BF16),
                        pltpu.VMEM((n_bd, rep * HEAD_DIM, rep * LANES), BF16),
                        pltpu.VMEM((n_bd, rep * LANES, rep * HEAD_DIM), BF16),
                        pltpu.VMEM((n_slots, qb, rep * LANES), F32)],
        compiler_params=_cparams(3),
        name="nsa_attention",
    )(q, gates, kct, vc, kt, v2, v2s, overlap_t, esel, expand)


def _oproj_kernel(x_ref, o_ref, w_ref, gpost_ref, out_ref):
    out_ref[...] = x_ref[...] + _rms(_dot(o_ref[...], w_ref[...]), gpost_ref[...])


def _oproj_call(x2d, o, w_o, gpost, tm):
    n, d = x2d.shape
    dq = o.shape[-1]
    row = lambda i: (i, 0)
    fix = lambda i: (0, 0)
    return pl.pallas_call(
        _oproj_kernel,
        grid=(n // tm,),
        in_specs=[pl.BlockSpec((tm, d), row), pl.BlockSpec((tm, dq), row),
                  pl.BlockSpec(w_o.shape, fix), pl.BlockSpec((1, d), fix)],
        out_specs=pl.BlockSpec((tm, d), row),
        out_shape=jax.ShapeDtypeStruct((n, d), F32),
        compiler_params=_cparams(1),
        name="nsa_out_proj",
    )(x2d, o, w_o, gpost.reshape(1, d))


def _shared_kv(x, kv_norm_g, w_kv, pe_k, pe_v, k_w1, k_w2, v_w1, v_w2):
    b, t, d = x.shape
    n = b * t
    g, hd = KV_HEADS, HEAD_DIM
    tm = _row_tile(t, 512)
    def cols(ta, tb):
        return np.concatenate([np.concatenate([np.arange(hd) + ta * g * hd + gi * hd,
                                               np.arange(hd) + tb * g * hd + gi * hd]) for gi in range(g)])
    k_cols = cols(2, 4)
    wkt = jnp.concatenate([w_kv[:, k_cols], w_kv[:, k_cols[_swap_perm(k_cols.size)]]], axis=1).T.astype(BF16)
    w_rest = jnp.concatenate([w_kv[:, cols(3, 5)], w_kv[:, cols(5, 3)], w_kv[:, cols(0, 1)]], axis=1).astype(BF16)
    cos, sin = _rope_tables(jnp.arange(t), LANES)
    kt, v2, v2s, ctok = _kv_call(x.reshape(n, d), kv_norm_g, wkt, w_rest, cos.T, sin.T, b, t, tm)
    v2 = v2.reshape(b, t, g * 2 * hd)
    v2s = v2s.reshape(b, t, g * 2 * hd)
    c = t // CMP_STRIDE
    n_cmp = c - CMP_BLOCK // CMP_STRIDE + 1
    assert CMP_BLOCK == 2 * CMP_STRIDE
    st, hid = CMP_STRIDE, k_w1.shape[-1]
    pe = jnp.stack([pe_k.reshape(2, st, hd), pe_v.reshape(2, st, hd)], axis=2).reshape(2, st * 2 * hd).astype(F32)
    zw = jnp.zeros((2, st, hd, hid), k_w1.dtype)
    w1 = jnp.stack([jnp.concatenate([k_w1.reshape(2, st, hd, hid), zw], axis=-1),
                    jnp.concatenate([zw, v_w1.reshape(2, st, hd, hid)], axis=-1)], axis=2)
    w1 = w1.reshape(2, st * 2 * hd, 2 * hid).astype(BF16)
    zpad = jnp.zeros_like(k_w2)
    w2kt = jnp.stack([jnp.concatenate([k_w2, zpad], axis=1).T,
                      jnp.concatenate([k_w2[:, _swap_perm(hd)], zpad], axis=1).T]).astype(BF16)
    w2v = jnp.concatenate([v_w2, v_w2], axis=1).astype(BF16)
    pos_c = jnp.arange(c) * CMP_STRIDE + CMP_BLOCK - 1
    cos_c, sin_c = _rope_tables(pos_c, hd)
    zlane = jnp.zeros((c, LANES - hd), F32)
    cos_c = jnp.concatenate([cos_c, zlane], axis=1).T
    sin_c = jnp.concatenate([sin_c, zlane], axis=1).T
    kct, vc = _cmp_call(ctok.reshape(b, t, g * 2 * hd), pe, w1, w2kt, w2v, cos_c, sin_c, n_cmp)
    return kct, vc, kt, v2, v2s, n_cmp


def _nsa_layer(x, gpre, gpost, w_q, w_o, kct, vc, kt, v2, v2s, n_cmp):
    b, t, d = x.shape
    n = b * t
    hd = HEAD_DIM
    dq = N_HEADS * hd
    tm = _row_tile(t, 512)
    wg = jnp.zeros((d, LANES), w_q.dtype).at[:, :3 * N_HEADS].set(w_q[:, dq:])
    w_all = jnp.concatenate([w_q[:, :dq], w_q[:, :dq][:, _swap_perm(dq)], wg], axis=1).astype(BF16)
    cos, sin = _rope_tables(jnp.arange(t), LANES)
    q, gates = _q_call(x.reshape(n, d), gpre, w_all, cos, sin, t, tm)
    n_sb = t // SEL_BLOCK
    assert n_sb % 8 == 0 and n_sb <= LANES
    c = vc.shape[2]
    ci = np.arange(c)[None, :]
    sj = np.arange(LANES)[:, None]
    overlap_t = ((ci * CMP_STRIDE < (sj + 1) * SEL_BLOCK) & (ci * CMP_STRIDE + CMP_BLOCK > sj * SEL_BLOCK)
                 & (ci < n_cmp) & (sj < n_sb)).astype(np.float32)
    esel = (np.arange(LANES)[:, None] == (np.arange(t)[None, :] // SEL_BLOCK)).astype(np.float32)
    gw = dq // KV_HEADS
    expand = np.zeros((KV_HEADS, LANES, 3 * gw), np.float32)
    for gi in range(KV_HEADS):
        hh = gi * (N_HEADS // KV_HEADS) + np.arange(gw) // hd
        for j in range(3):
            expand[gi, hh * 3 + j, j * gw + np.arange(gw)] = 1.0
    o = _attn_call(q.reshape(b, t, dq), gates.reshape(b, t, LANES), kct, vc, kt, v2, v2s,
                   jnp.asarray(overlap_t, dtype=BF16), jnp.asarray(esel, dtype=BF16),
                   jnp.asarray(expand, dtype=BF16), n_sb)
    out = _oproj_call(x.reshape(n, d), o.reshape(n, dq), w_o.astype(BF16), gpost, tm)
    return out.reshape(b, t, d)


def kernel(x, a_lam_re, a_lam_im, a_log_dt, a_b_re, a_b_im, a_c_re, a_c_im, a_d, a_w_glu, b_w_q, b_w_o, kv_norm_g, w_kv, cmp_pe_k, cmp_pe_v, cmp_k_w1, cmp_k_w2, cmp_v_w1, cmp_v_w2, mix_pre_g, mix_post_g, ffn_pre_g, ffn_post_g, ffn_w_in, ffn_conv_w, ffn_conv_b, ffn_w_out):
    depth = mix_pre_g.shape[0]
    n_a = depth // 2
    kv = None
    for layer in range(depth):
        if layer < n_a:
            i = layer
            x = _s5_layer(x, mix_pre_g[layer], mix_post_g[layer], a_lam_re[i], a_lam_im[i], a_log_dt[i],
                          a_b_re[i], a_b_im[i], a_c_re[i], a_c_im[i], a_d[i], a_w_glu[i])
        else:
            j = layer - n_a
            x = _nsa_layer(x, mix_pre_g[layer], mix_post_g[layer], b_w_q[j], b_w_o[j], *kv)
        x = _ffn_layer(x, ffn_pre_g[layer], ffn_post_g[layer], ffn_w_in[layer], ffn_conv_w[layer],
                       ffn_conv_b[layer], ffn_w_out[layer])
        if layer == n_a - 1:
            kv = _shared_kv(x, kv_norm_g, w_kv, cmp_pe_k, cmp_pe_v, cmp_k_w1, cmp_k_w2, cmp_v_w1, cmp_v_w2)
    return x
```

```python
import functools
import math

import numpy as np
import jax
import jax.numpy as jnp
from jax import lax
from jax.experimental import pallas as pl
from jax.experimental.pallas import tpu as pltpu

F32 = jnp.float32
BF16 = jnp.bfloat16

S5_GROUP = 16
S5_STATE = 64
N_HEADS = 16
KV_HEADS = 4
HEAD_DIM = 64
CMP_BLOCK = 32
CMP_STRIDE = 16
SEL_BLOCK = 64
SEL_TOPN = 16
WINDOW = 512
Q_BLOCK = 128
ROPE_THETA = 500000.0
ROPE_DIM = HEAD_DIM // 4
CONV_WIDTH = 3
EPS = 1e-6
NEG = -1e30
SEL_FORCE = 1e4
SEL_NEG = -1e4
LOG2E = math.log2(math.e)

LANES = 128
S5_CHUNK = 16
S5_GROUPS_PER_STEP = 8
CONV_HALO = 16
FFN_ROWS = 1024
FFN_HIDDEN_CAP = 1408
VMEM_LIMIT = 48 * 1024 * 1024


def _cparams(n_axes):
    return pltpu.CompilerParams(dimension_semantics=("arbitrary",) * n_axes,
                                vmem_limit_bytes=VMEM_LIMIT)


def _rms(x, g):
    return x * lax.rsqrt(jnp.mean(x * x, axis=-1, keepdims=True) + EPS) * g


def _gelu(x):
    return jax.nn.gelu(x, approximate=True)


def _dot(a, b):
    return jnp.dot(a, b, preferred_element_type=F32)


def _dot_nt(a, b):
    return lax.dot_general(a, b, (((1,), (1,)), ((), ())), preferred_element_type=F32)


def _row_tile(n, want):
    t = min(n, want)
    assert n % t == 0
    return t


def _s5_tables(lam_re, lam_im, log_dt, b_re, b_im, c_re, c_im, n_chunks):
    hp = lax.Precision.HIGHEST
    L = S5_CHUNK
    G, P = lam_re.shape
    I = b_re.shape[-1]
    dt = jnp.exp(log_dt.astype(F32))[:, None]
    lr, li = lam_re.astype(F32), lam_im.astype(F32)
    mag = jnp.exp(lr * dt)
    ab_re, ab_im = mag * jnp.cos(li * dt), mag * jnp.sin(li * dt)
    nr, ni = ab_re - 1.0, ab_im
    den = lr * lr + li * li
    coef_re = (nr * lr + ni * li) / den
    coef_im = (ni * lr - nr * li) / den
    br, bi = b_re.astype(F32), b_im.astype(F32)
    bb_re = coef_re[..., None] * br - coef_im[..., None] * bi
    bb_im = coef_re[..., None] * bi + coef_im[..., None] * br
    pr = [jnp.ones_like(ab_re)]
    pi = [jnp.zeros_like(ab_re)]
    for _ in range(L):
        r, i = pr[-1], pi[-1]
        pr.append(r * ab_re - i * ab_im)
        pi.append(r * ab_im + i * ab_re)
    pw_re = jnp.stack(pr)
    pw_im = jnp.stack(pi)
    cr, ci = c_re.astype(F32), c_im.astype(F32)
    cl_re = cr[None] * pw_re[:, :, None, :] - ci[None] * pw_im[:, :, None, :]
    cl_im = cr[None] * pw_im[:, :, None, :] + ci[None] * pw_re[:, :, None, :]
    kk = jnp.einsum('kgop,gpi->gkio', jnp.concatenate([cl_re[:L], -cl_im[:L]], axis=-1),
                    jnp.concatenate([bb_re, bb_im], axis=1), precision=hp)
    rev_re = pw_re[L - 1 - np.arange(L)]
    rev_im = pw_im[L - 1 - np.arange(L)]
    pb_re = rev_re[..., None] * bb_re[None] - rev_im[..., None] * bb_im[None]
    pb_im = rev_re[..., None] * bb_im[None] + rev_im[..., None] * bb_re[None]
    gl = LANES // I
    nt = G // gl
    ka = kk.reshape(nt, gl, L, I, I).transpose(0, 2, 1, 3, 4).reshape(nt, L * LANES, I).astype(BF16)
    ps = jnp.concatenate([pb_re, pb_im], axis=2).reshape(L, nt, gl, 2 * P, I)
    ps = ps.transpose(1, 0, 2, 4, 3).reshape(nt, L * LANES, 2 * P).astype(BF16)
    qt = jnp.concatenate([cl_re[1:], -cl_im[1:]], axis=-1).reshape(L, nt, gl, I, 2 * P)
    qt = qt.transpose(1, 0, 2, 3, 4).reshape(nt, L * LANES, 2 * P).astype(BF16)
    n_steps = int(math.ceil(math.log2(n_chunks))) if n_chunks > 1 else 0
    mr, mi = pw_re[L], pw_im[L]
    a1, a2 = [], []
    for _ in range(max(n_steps, 1)):
        a1.append(jnp.concatenate([mr, mr], axis=-1).reshape(nt, gl * 2 * P))
        a2.append(jnp.concatenate([-mi, mi], axis=-1).reshape(nt, gl * 2 * P))
        mr, mi = mr * mr - mi * mi, 2.0 * mr * mi
    pad = (-len(a1)) % 8
    a1 = jnp.stack(a1 + [jnp.zeros_like(a1[0])] * pad, axis=1)
    a2 = jnp.stack(a2 + [jnp.zeros_like(a2[0])] * pad, axis=1)
    return ka, ps, qt, a1, a2, n_steps


def _norm_kernel(x_ref, g_ref, o_ref):
    o_ref[...] = _rms(x_ref[...], g_ref[...]).astype(o_ref.dtype)


def _norm_call(x2d, g, out_dtype, tm):
    n, d = x2d.shape
    return pl.pallas_call(
        _norm_kernel,
        grid=(n // tm,),
        in_specs=[pl.BlockSpec((tm, d), lambda i: (i, 0)),
                  pl.BlockSpec((1, d), lambda i: (0, 0))],
        out_specs=pl.BlockSpec((tm, d), lambda i: (i, 0)),
        out_shape=jax.ShapeDtypeStruct((n, d), out_dtype),
        compiler_params=_cparams(1),
        name="s5_prenorm",
    )(x2d, g.reshape(1, d))


def _s5_scan_kernel(u_ref, ka_ref, ps_ref, qt_ref, rep_ref, own_ref, a1_ref, a2_ref, y_ref,
                    wy_ref, ws_ref, wq_ref, *, n_steps):
    L = S5_CHUNK
    I = S5_GROUP
    gl = LANES // I
    n_chunks = u_ref.shape[1] // L

    @pl.when(pl.program_id(1) == 0)
    def _():
        lag = _dot(ka_ref[0], rep_ref[...])
        own = own_ref[...]
        lag = [(lag[LANES * k:LANES * (k + 1)] * own).astype(BF16) for k in range(L)]
        wy_ref[...] = jnp.zeros_like(wy_ref)
        ws_ref[...] = jnp.zeros_like(ws_ref)
        wq_ref[...] = jnp.zeros_like(wq_ref)
        for s in range(L):
            for r in range(s, L):
                wy_ref[LANES * s:LANES * (s + 1), LANES * r:LANES * (r + 1)] = lag[r - s]
            for g in range(gl):
                rows = slice(LANES * s + I * g, LANES * s + I * (g + 1))
                ws_ref[rows, LANES * g:LANES * (g + 1)] = ps_ref[0, rows, :]
                wq_ref[rows, LANES * g:LANES * (g + 1)] = qt_ref[0, rows, :]

    ucat = jnp.concatenate([u_ref[0, pl.ds(s, n_chunks, stride=L), :] for s in range(L)], axis=1).astype(BF16)
    y_intra = _dot(ucat, wy_ref[...])
    ends = _dot(ucat, ws_ref[...])
    row = lax.broadcasted_iota(jnp.int32, (n_chunks, LANES), 0)
    xprev = []
    for g in range(gl):
        x = ends[:, LANES * g:LANES * (g + 1)]
        for j in range(n_steps):
            k = 1 << j
            sh = jnp.where(row >= k, pltpu.roll(x, k, 0), 0.0)
            x = (x + a1_ref[0, j:j + 1, LANES * g:LANES * (g + 1)] * sh
                 + a2_ref[0, j:j + 1, LANES * g:LANES * (g + 1)] * pltpu.roll(sh, S5_STATE, 1))
        xprev.append(jnp.where(row >= 1, pltpu.roll(x, 1, 0), 0.0).astype(BF16))
    y = y_intra + _dot_nt(jnp.concatenate(xprev, axis=1), wq_ref[...])
    for s in range(L):
        y_ref[0, pl.ds(s, n_chunks, stride=L), :] = y[:, LANES * s:LANES * (s + 1)]


def _s5_scan_call(u, ka, ps, qt, a1, a2, n_steps):
    b, t, d = u.shape
    nt = d // LANES
    gl = LANES // S5_GROUP
    wide = S5_CHUNK * LANES
    lane = np.arange(LANES)
    rep = jnp.asarray(lane[None, :] % S5_GROUP == np.arange(S5_GROUP)[:, None], BF16)
    own = jnp.asarray(lane[:, None] // S5_GROUP == lane[None, :] // S5_GROUP, F32)
    per_tile = lambda j, i: (j, 0, 0)
    fix = lambda j, i: (0, 0)
    return pl.pallas_call(
        functools.partial(_s5_scan_kernel, n_steps=n_steps),
        grid=(nt, b),
        in_specs=[pl.BlockSpec((1, t, LANES), lambda j, i: (i, 0, j)),
                  pl.BlockSpec((1,) + ka.shape[1:], per_tile),
                  pl.BlockSpec((1,) + ps.shape[1:], per_tile),
                  pl.BlockSpec((1,) + qt.shape[1:], per_tile),
                  pl.BlockSpec(rep.shape, fix), pl.BlockSpec(own.shape, fix),
                  pl.BlockSpec((1,) + a1.shape[1:], per_tile),
                  pl.BlockSpec((1,) + a2.shape[1:], per_tile)],
        out_specs=pl.BlockSpec((1, t, LANES), lambda j, i: (i, 0, j)),
        out_shape=jax.ShapeDtypeStruct((b, t, d), F32),
        scratch_shapes=[pltpu.VMEM((wide, wide), BF16),
                        pltpu.VMEM((wide, gl * 2 * S5_STATE), BF16),
                        pltpu.VMEM((wide, gl * 2 * S5_STATE), BF16)],
        compiler_params=_cparams(2),
        name="s5_scan",
    )(u, ka, ps, qt, rep, own, a1, a2)


def _s5_out_kernel(x_ref, y_ref, gpre_ref, d_ref, w_ref, gpost_ref, o_ref):
    x = x_ref[...]
    u = _rms(x, gpre_ref[...])
    z = _gelu(y_ref[...] + u * d_ref[...]).astype(BF16)
    ag = _dot(z, w_ref[...])
    d = x.shape[-1]
    m = ag[:, :d] * jax.nn.sigmoid(ag[:, d:])
    o_ref[...] = x + _rms(m, gpost_ref[...])


def _s5_out_call(x2d, y2d, gpre, dskip, wglu, gpost, tm):
    n, d = x2d.shape
    row = lambda i: (i, 0)
    fix = lambda i: (0, 0)
    return pl.pallas_call(
        _s5_out_kernel,
        grid=(n // tm,),
        in_specs=[pl.BlockSpec((tm, d), row), pl.BlockSpec((tm, d), row),
                  pl.BlockSpec((1, d), fix), pl.BlockSpec((1, d), fix),
                  pl.BlockSpec(wglu.shape, fix), pl.BlockSpec((1, d), fix)],
        out_specs=pl.BlockSpec((tm, d), row),
        out_shape=jax.ShapeDtypeStruct((n, d), F32),
        compiler_params=_cparams(1),
        name="s5_glu_out",
    )(x2d, y2d, gpre.reshape(1, d), dskip.reshape(1, d), wglu, gpost.reshape(1, d))


def _s5_layer(x, gpre, gpost, lam_re, lam_im, log_dt, b_re, b_im, c_re, c_im, d_skip, w_glu):
    b, t, d = x.shape
    n = b * t
    tm = _row_tile(n, 512)
    assert t % (8 * S5_CHUNK) == 0 and d % LANES == 0 and LANES % S5_GROUP == 0
    ka, ps, qt, a1, a2, n_steps = _s5_tables(lam_re, lam_im, log_dt, b_re, b_im, c_re, c_im, t // S5_CHUNK)
    x2d = x.reshape(n, d)
    u = _norm_call(x2d, gpre, F32, tm)
    y = _s5_scan_call(u.reshape(b, t, d), ka, ps, qt, a1, a2, n_steps)
    out = _s5_out_call(x2d, y.reshape(n, d), gpre, d_skip, w_glu.astype(BF16), gpost, tm)
    return out.reshape(b, t, d)


def _ffn_kernel(xprev_ref, x_ref, gpre_ref, wg_ref, wv_ref, cw_ref, cb_ref, wo_ref, gpost_ref,
                o_ref, xn_ref, acc_ref, *, seq_tiles):
    i = pl.program_id(0)
    j = pl.program_id(1)
    h = CONV_HALO

    @pl.when(j == 0)
    def _():
        g = gpre_ref[...]
        xn_ref[h:, :] = _rms(x_ref[...], g).astype(BF16)
        keep = jnp.where(i % seq_tiles == 0, 0.0, 1.0)
        xn_ref[:h, :] = (_rms(xprev_ref[...], g) * keep).astype(BF16)
        acc_ref[...] = jnp.zeros_like(acc_ref)

    xn = xn_ref[...]
    gate = _dot(xn, wg_ref[...])
    val = _dot(xn[h:], wv_ref[...])
    cw = cw_ref[...]
    conv = (cw[0:1] * pltpu.roll(gate, 2, 0) + cw[1:2] * pltpu.roll(gate, 1, 0)
            + cw[2:3] * gate + cb_ref[...])
    act = _gelu(conv[h:]) * val
    acc_ref[...] += _dot(act.astype(BF16), wo_ref[...])

    @pl.when(j == pl.num_programs(1) - 1)
    def _():
        o_ref[...] = x_ref[...] + _rms(acc_ref[...], gpost_ref[...])


def _ffn_layer(x, gpre, gpost, w_in, conv_w, conv_b, w_out):
    b, t, d = x.shape
    f = w_out.shape[0]
    n = b * t
    tm = _row_tile(t, FFN_ROWS)
    tf = max(w for w in range(LANES, min(f, FFN_HIDDEN_CAP) + 1, LANES) if f % w == 0)
    assert f % tf == 0 and tm % CONV_HALO == 0
    nf = f // tf
    h = CONV_HALO
    x2d = x.reshape(n, d)
    cw = jnp.zeros((8, f), F32).at[:CONV_WIDTH].set(conv_w.astype(F32))
    w_in16 = w_in.astype(BF16)
    out = pl.pallas_call(
        functools.partial(_ffn_kernel, seq_tiles=t // tm),
        grid=(n // tm, nf),
        in_specs=[pl.BlockSpec((h, d), lambda i, j: (jnp.maximum(i * (tm // h) - 1, 0), 0)),
                  pl.BlockSpec((tm, d), lambda i, j: (i, 0)),
                  pl.BlockSpec((1, d), lambda i, j: (0, 0)),
                  pl.BlockSpec((d, tf), lambda i, j: (0, j)),
                  pl.BlockSpec((d, tf), lambda i, j: (0, nf + j)),
                  pl.BlockSpec((8, tf), lambda i, j: (0, j)),
                  pl.BlockSpec((1, tf), lambda i, j: (0, j)),
                  pl.BlockSpec((tf, d), lambda i, j: (j, 0)),
                  pl.BlockSpec((1, d), lambda i, j: (0, 0))],
        out_specs=pl.BlockSpec((tm, d), lambda i, j: (i, 0)),
        out_shape=jax.ShapeDtypeStruct((n, d), F32),
        scratch_shapes=[pltpu.VMEM((tm + h, d), BF16), pltpu.VMEM((tm, d), F32)],
        compiler_params=_cparams(2),
        name="conv_ffn",
    )(x2d, x2d, gpre.reshape(1, d), w_in16, w_in16, cw, conv_b.reshape(1, f).astype(F32),
      w_out.astype(BF16), gpost.reshape(1, d))
    return out.reshape(b, t, d)


def _rope_tables(pos, width):
    half = ROPE_DIM // 2
    inv = ROPE_THETA ** (-jnp.arange(half, dtype=F32) / half)
    ang = pos.astype(F32)[:, None] * inv[None, :]
    cos, sin = jnp.cos(ang), jnp.sin(ang)
    rest = HEAD_DIM - ROPE_DIM
    n = pos.shape[0]
    c = jnp.concatenate([cos, cos, jnp.ones((n, rest), F32)], axis=-1)
    s = jnp.concatenate([-sin, sin, jnp.zeros((n, rest), F32)], axis=-1)
    reps = width // HEAD_DIM
    return jnp.tile(c, (1, reps)), jnp.tile(s, (1, reps))


def _swap_perm(width):
    half = ROPE_DIM // 2
    d = np.arange(width)
    dd = d % HEAD_DIM
    return np.where(dd < half, d + half, np.where(dd < ROPE_DIM, d - half, d))


def _kv_kernel(x_ref, g_ref, wkt_ref, w_ref, cost_ref, sint_ref, kt_ref, v_ref, vs_ref, c_ref):
    wk = v_ref.shape[-1]
    sn = _rms(x_ref[...], g_ref[...]).astype(BF16)
    rt = _dot_nt(wkt_ref[...], sn)
    reps = wk // LANES
    cos = jnp.concatenate([cost_ref[...]] * reps, axis=0)
    sin = jnp.concatenate([sint_ref[...]] * reps, axis=0)
    kt_ref[0] = (rt[:wk] * cos + rt[wk:] * sin).astype(kt_ref.dtype)
    r = _dot(sn, w_ref[...])
    v_ref[...] = r[:, :wk].astype(v_ref.dtype)
    vs_ref[...] = r[:, wk:2 * wk].astype(vs_ref.dtype)
    c_ref[...] = r[:, 2 * wk:]


def _kv_call(x2d, g, wkt, w_rest, cos_t, sin_t, b, t, tm):
    n, d = x2d.shape
    wk = KV_HEADS * 2 * HEAD_DIM
    tpt = t // tm
    row = lambda i: (i, 0)
    fix = lambda i: (0, 0)
    tab = lambda i: (0, i % tpt)
    return pl.pallas_call(
        _kv_kernel,
        grid=(n // tm,),
        in_specs=[pl.BlockSpec((tm, d), row), pl.BlockSpec((1, d), fix),
                  pl.BlockSpec(wkt.shape, fix), pl.BlockSpec(w_rest.shape, fix),
                  pl.BlockSpec((LANES, tm), tab), pl.BlockSpec((LANES, tm), tab)],
        out_specs=[pl.BlockSpec((1, wk, tm), lambda i: (i // tpt, 0, i % tpt)),
                   pl.BlockSpec((tm, wk), row), pl.BlockSpec((tm, wk), row), pl.BlockSpec((tm, wk), row)],
        out_shape=[jax.ShapeDtypeStruct((b, wk, t), BF16), jax.ShapeDtypeStruct((n, wk), BF16),
                   jax.ShapeDtypeStruct((n, wk), BF16), jax.ShapeDtypeStruct((n, wk), F32)],
        compiler_params=_cparams(1),
        name="nsa_kv_proj",
    )(x2d, g.reshape(1, d), wkt, w_rest, cos_t, sin_t)


def _cmp_kernel(z_ref, pe_ref, w1_ref, w2kt_ref, w2v_ref, cost_ref, sint_ref, kt_ref, v_ref, *, n_cmp):
    st = CMP_STRIDE
    c = z_ref.shape[1] // st
    hid = w1_ref.shape[-1] // 2
    z = jnp.concatenate([z_ref[0, pl.ds(l, c, stride=st), :] for l in range(st)], axis=1)
    top = _dot((z + pe_ref[0:1, :]).astype(BF16), w1_ref[0])
    bot = _dot((z + pe_ref[1:2, :]).astype(BF16), w1_ref[1])
    hdn = _gelu(top + pltpu.roll(bot, c - 1, 0)).astype(BF16)
    res = [hdn[:, :hid], hdn[:, hid:]]
    kt = (_dot_nt(w2kt_ref[0], res[0]) * cost_ref[...] + _dot_nt(w2kt_ref[1], res[0]) * sint_ref[...])
    col = lax.broadcasted_iota(jnp.int32, kt.shape, 1)
    kt_ref[0, 0] = jnp.where(col < n_cmp, kt, 0.0).astype(kt_ref.dtype)
    v = _dot(res[1], w2v_ref[...])
    row = lax.broadcasted_iota(jnp.int32, v.shape, 0)
    v_ref[0, 0] = jnp.where(row < n_cmp, v, 0.0).astype(v_ref.dtype)


def _cmp_call(ctok, pe, w1, w2kt, w2v, cos_t, sin_t, n_cmp):
    b, t, wid = ctok.shape
    g = wid // LANES
    c = t // CMP_STRIDE
    fix2 = lambda i, j: (0, 0)
    fix3 = lambda i, j: (0, 0, 0)
    return pl.pallas_call(
        functools.partial(_cmp_kernel, n_cmp=n_cmp),
        grid=(b, g),
        in_specs=[pl.BlockSpec((1, t, LANES), lambda i, j: (i, 0, j)),
                  pl.BlockSpec(pe.shape, fix2), pl.BlockSpec(w1.shape, fix3),
                  pl.BlockSpec(w2kt.shape, fix3), pl.BlockSpec(w2v.shape, fix2),
                  pl.BlockSpec(cos_t.shape, fix2), pl.BlockSpec(sin_t.shape, fix2)],
        out_specs=[pl.BlockSpec((1, 1, LANES, c), lambda i, j: (i, j, 0, 0)),
                   pl.BlockSpec((1, 1, c, LANES), lambda i, j: (i, j, 0, 0))],
        out_shape=[jax.ShapeDtypeStruct((b, g, LANES, c), BF16), jax.ShapeDtypeStruct((b, g, c, LANES), BF16)],
        compiler_params=_cparams(2),
        name="nsa_compress",
    )(ctok, pe, w1, w2kt, w2v, cos_t, sin_t)


def _q_kernel(x_ref, g_ref, w_ref, cos_ref, sin_ref, q_ref, gate_ref):
    d = q_ref.shape[-1]
    hn = _rms(x_ref[...], g_ref[...]).astype(BF16)
    r = _dot(hn, w_ref[...])
    reps = d // LANES
    cos = jnp.concatenate([cos_ref[...]] * reps, axis=1)
    sin = jnp.concatenate([sin_ref[...]] * reps, axis=1)
    q_ref[...] = ((r[:, :d] * cos + r[:, d:2 * d] * sin) * (HEAD_DIM ** -0.5 * LOG2E)).astype(q_ref.dtype)
    gate_ref[...] = jax.nn.sigmoid(r[:, 2 * d:])


def _q_call(x2d, g, w_all, cos, sin, t, tm):
    n, d = x2d.shape
    dq = N_HEADS * HEAD_DIM
    tpt = t // tm
    row = lambda i: (i, 0)
    fix = lambda i: (0, 0)
    tab = lambda i: (i % tpt, 0)
    return pl.pallas_call(
        _q_kernel,
        grid=(n // tm,),
        in_specs=[pl.BlockSpec((tm, d), row), pl.BlockSpec((1, d), fix),
                  pl.BlockSpec(w_all.shape, fix),
                  pl.BlockSpec((tm, LANES), tab), pl.BlockSpec((tm, LANES), tab)],
        out_specs=[pl.BlockSpec((tm, dq), row), pl.BlockSpec((tm, LANES), row)],
        out_shape=[jax.ShapeDtypeStruct((n, dq), BF16), jax.ShapeDtypeStruct((n, LANES), F32)],
        compiler_params=_cparams(1),
        name="nsa_q_proj",
    )(x2d, g.reshape(1, d), w_all, cos, sin)


def _attn_kernel_v1(q_ref, kvc_ref, k2_ref, v2_ref, ov_ref, esel_ref, oc_ref, os_ref, ow_ref,
                    selexp_ref, *, n_sb, top_n):
    qb = q_ref.shape[1]
    rep = N_HEADS // KV_HEADS
    hd = HEAD_DIM
    i = pl.program_id(2)
    start = i * qb
    lane = lax.broadcasted_iota(jnp.int32, (qb, LANES), 1)
    lo = lane < hd

    q = q_ref[0].astype(F32)
    zero = jnp.zeros((qb, LANES), F32)
    q_lo, q_hi = [], []
    for p in range(rep * hd // LANES):
        part = q[:, p * LANES:(p + 1) * LANES]
        swapped = pltpu.roll(part, hd, 1)
        q_lo += [jnp.where(lo, part, zero), jnp.where(lo, swapped, zero)]
        q_hi += [jnp.where(lo, zero, swapped), jnp.where(lo, zero, part)]
    q_lo = jnp.concatenate(q_lo, axis=0).astype(BF16)
    q_hi = jnp.concatenate(q_hi, axis=0).astype(BF16)
    rows = rep * qb
    t_rows = start + (lax.broadcasted_iota(jnp.int32, (rows, LANES), 0) & (qb - 1))
    lane_r = lax.broadcasted_iota(jnp.int32, (rows, LANES), 1)

    def unstack(acc, valid_low):
        outs = []
        for p in range(rep // 2):
            a0 = acc[(2 * p) * qb:(2 * p + 1) * qb]
            a1 = acc[(2 * p + 1) * qb:(2 * p + 2) * qb]
            if valid_low:
                outs.append(jnp.where(lo, a0, pltpu.roll(a1, hd, 1)))
            else:
                outs.append(jnp.where(lo, pltpu.roll(a0, hd, 1), a1))
        return jnp.concatenate(outs, axis=1)

    kvc = kvc_ref[0, 0]
    n_c = kvc.shape[0]
    s1 = _dot_nt(q_lo, kvc)
    n_idx = lax.broadcasted_iota(jnp.int32, (rows, n_c), 1)
    t1 = start + (lax.broadcasted_iota(jnp.int32, (rows, n_c), 0) & (qb - 1))
    m1 = n_idx * CMP_STRIDE + (CMP_BLOCK - 1) <= t1
    s1m = jnp.where(m1, s1, NEG)
    e1 = jnp.where(m1, jnp.exp(s1m - jnp.max(s1m, axis=1, keepdims=True)), 0.0)
    l1 = jnp.sum(e1, axis=1, keepdims=True)
    p1 = e1 / jnp.where(l1 > 0.0, l1, 1.0)
    oc_ref[0] = unstack(_dot(p1.astype(BF16), kvc), False)

    psum = p1[0:qb]
    for r in range(1, rep):
        psum = psum + p1[r * qb:(r + 1) * qb]
    imp = jnp.dot(psum, ov_ref[...], preferred_element_type=F32, precision=lax.Precision.HIGHEST)
    t_q = start + lax.broadcasted_iota(jnp.int32, (qb, LANES), 0)
    cur = jnp.right_shift(t_q, int(math.log2(SEL_BLOCK)))
    valid = lane <= cur
    forced = (lane == 0) | (lane == cur) | (lane == cur - 1)
    score = jnp.where(forced, SEL_FORCE, jnp.where(valid, imp, SEL_NEG))
    score = jnp.where(lane < n_sb, score, -jnp.inf)
    lane_f = lane.astype(F32)
    work = score
    sel = jnp.zeros((qb, LANES), F32)
    for _ in range(top_n):
        mx = jnp.max(work, axis=1, keepdims=True)
        first = jnp.min(jnp.where(work == mx, lane_f, float(LANES)), axis=1, keepdims=True)
        hit = lane_f == first
        sel = jnp.where(hit, 1.0, sel)
        work = jnp.where(hit, -jnp.inf, work)
    sel = jnp.where(score > 0.5 * SEL_NEG, sel, 0.0)
    t_len = selexp_ref.shape[1]
    key = lax.broadcasted_iota(jnp.int32, (qb, t_len), 1)
    t_k = start + lax.broadcasted_iota(jnp.int32, (qb, t_len), 0)
    selexp_ref[...] = jnp.where(key <= t_k, _dot(sel.astype(BF16), esel_ref[...]), 0.0)

    def flash_step(qmat, kt, mask, carry):
        m, l, acc = carry
        off = pl.multiple_of(kt * LANES, LANES)
        s = _dot_nt(qmat, k2_ref[0, pl.ds(off, LANES), :])
        sm = jnp.where(mask, s, NEG)
        m_new = jnp.maximum(m, jnp.max(sm, axis=1, keepdims=True))
        alpha = jnp.exp(m - m_new)
        p = jnp.where(mask, jnp.exp(sm - m_new), 0.0)
        l = alpha * l + jnp.sum(p, axis=1, keepdims=True)
        acc = alpha * acc + _dot(p.astype(BF16), v2_ref[0, pl.ds(off, LANES), :])
        return m_new, l, acc

    init = (jnp.full((rows, 1), NEG, F32), jnp.zeros((rows, 1), F32), jnp.zeros((rows, LANES), F32))

    def sel_body(kt, carry):
        off = pl.multiple_of(kt * LANES, LANES)
        mk = selexp_ref[:, pl.ds(off, LANES)]
        mask = jnp.concatenate([mk] * rep, axis=0) > 0.5
        return flash_step(q_lo, kt, mask, carry)

    _, l2, acc2 = lax.fori_loop(0, i + 1, sel_body, init)
    os_ref[0] = unstack(acc2 / l2, True)

    def win_body(kt, carry):
        kpos = kt * LANES + lane_r
        mask = (kpos <= t_rows) & (t_rows - kpos < WINDOW)
        return flash_step(q_hi, kt, mask, carry)

    n_win = WINDOW // LANES
    _, l3, acc3 = lax.fori_loop(jnp.maximum(i - n_win, 0), i + 1, win_body, init)
    ow_ref[0] = unstack(acc3 / l3, False)


def _attn_call_v1(q, kvc, k2, v2, overlap, esel, n_sb):
    b, t, dq = q.shape
    g = KV_HEADS
    gw = dq // g
    c = kvc.shape[2]
    qb = Q_BLOCK
    assert qb == LANES and t % qb == 0
    blk_q = pl.BlockSpec((1, qb, gw), lambda bi, gi, i: (bi, i, gi))
    blk_kv = pl.BlockSpec((1, t, LANES), lambda bi, gi, i: (bi, 0, gi))
    out = jax.ShapeDtypeStruct((b, t, dq), F32)
    return pl.pallas_call(
        functools.partial(_attn_kernel, n_sb=n_sb, top_n=min(SEL_TOPN, n_sb)),
        grid=(b, g, t // qb),
        in_specs=[blk_q,
                  pl.BlockSpec((1, 1, c, LANES), lambda bi, gi, i: (bi, gi, 0, 0)),
                  blk_kv, blk_kv,
                  pl.BlockSpec(overlap.shape, lambda bi, gi, i: (0, 0)),
                  pl.BlockSpec(esel.shape, lambda bi, gi, i: (0, 0))],
        out_specs=[blk_q, blk_q, blk_q],
        out_shape=[out, out, out],
        scratch_shapes=[pltpu.VMEM((qb, t), F32)],
        compiler_params=_cparams(3),
        name="nsa_attention",
    )(q, kvc, k2, v2, overlap, esel)


ATTN_UNROLL = 4
ATTN_Q_ROWS = 256


def _attn_kernel(q_ref, gate_ref, kct_ref, vc_ref, kt_ref, v2_ref, v2s_ref, ovt_ref, esel_ref, eg_ref,
                 o_ref, kcbd_ref, vcbd_ref, kbd_ref, vbd_ref, sc_ref, *, n_sb, top_n):
    qb = q_ref.shape[1]
    rep = N_HEADS // KV_HEADS
    hd = HEAD_DIM
    n_c = kct_ref.shape[-1]
    n_tiles = kt_ref.shape[-1] // LANES
    i = pl.program_id(2)
    start = i * qb

    @pl.when(i == 0)
    def _():
        kcbd_ref[...] = jnp.zeros_like(kcbd_ref)
        vcbd_ref[...] = jnp.zeros_like(vcbd_ref)
        kbd_ref[...] = jnp.zeros_like(kbd_ref)
        vbd_ref[...] = jnp.zeros_like(vbd_ref)
        vc = vc_ref[0, 0]
        for r in range(rep):
            h = hd * (r % 2)
            kcbd_ref[hd * r:hd * (r + 1), n_c * r:n_c * (r + 1)] = kct_ref[0, 0, 0:hd, :]
            vcbd_ref[n_c * r:n_c * (r + 1), hd * r:hd * (r + 1)] = vc[:, h:h + hd]

    qblk = q_ref[0]

    s1 = _dot(qblk, kcbd_ref[...])
    n_idx = lax.broadcasted_iota(jnp.int32, (qb, n_c), 1)
    t1 = start + lax.broadcasted_iota(jnp.int32, (qb, n_c), 0)
    m1 = n_idx * CMP_STRIDE + (CMP_BLOCK - 1) <= t1
    psum = jnp.zeros((qb, n_c), F32)
    p_parts = []
    for r in range(rep):
        sm = jnp.where(m1, s1[:, n_c * r:n_c * (r + 1)], NEG)
        mrow = jnp.max(sm, axis=1, keepdims=True)
        e = jnp.exp2(sm - jnp.where(mrow > 0.5 * NEG, mrow, 0.0))
        l = jnp.sum(e, axis=1, keepdims=True)
        p = e / jnp.where(l > 0.0, l, 1.0)
        psum = psum + p
        p_parts.append(p.astype(BF16))
    o_c = _dot(jnp.concatenate(p_parts, axis=1), vcbd_ref[...])

    p_hi = psum.astype(BF16)
    p_lo = (psum - p_hi.astype(F32)).astype(BF16)
    imp_t = _dot_nt(ovt_ref[...], p_hi) + _dot_nt(ovt_ref[...], p_lo)
    blk = lax.broadcasted_iota(jnp.int32, (n_sb, qb), 0)
    t_l = start + lax.broadcasted_iota(jnp.int32, (n_sb, qb), 1)
    cur = jnp.right_shift(t_l, int(math.log2(SEL_BLOCK)))
    forced = (blk == 0) | (blk == cur) | (blk == cur - 1)
    score = jnp.where(forced, SEL_FORCE, jnp.where(blk <= cur, imp_t[:n_sb], SEL_NEG))
    blk_f = blk.astype(F32)
    work = score
    sel_t = jnp.zeros((n_sb, qb), F32)
    for _ in range(top_n):
        best = jnp.max(work, axis=0, keepdims=True)
        first = jnp.min(jnp.where(work == best, blk_f, float(n_sb)), axis=0, keepdims=True)
        hit = blk_f == first
        sel_t = jnp.where(hit, 1.0, sel_t)
        work = jnp.where(hit, -jnp.inf, work)
    sel_t = jnp.where(score > 0.5 * SEL_NEG, sel_t, 0.0)
    sel = sel_t.T.astype(BF16)

    lane = lax.broadcasted_iota(jnp.int32, (qb, LANES), 1)
    t_q = start + lax.broadcasted_iota(jnp.int32, (qb, LANES), 0)

    def scores(u, off, row0):
        ktile = kt_ref[0, row0:row0 + hd, pl.ds(off, LANES)]
        for r in range(rep):
            kbd_ref[u, hd * r:hd * (r + 1), LANES * r:LANES * (r + 1)] = ktile
        return _dot(qblk, kbd_ref[u])

    def weighted_values(u, off, ps, v_even, v_odd):
        ve = v_even[0, pl.ds(off, LANES), :]
        vo = v_odd[0, pl.ds(off, LANES), :]
        for r in range(rep):
            h = hd * (r % 2)
            src = ve if r % 2 == 0 else vo
            vbd_ref[u, LANES * r:LANES * (r + 1), hd * r:hd * (r + 1)] = src[:, h:h + hd]
        return _dot(jnp.concatenate(ps, axis=1), vbd_ref[u])

    def seg(a, r):
        return a[:, LANES * r:LANES * (r + 1)]

    def row_bcast(parts, op):
        return [jnp.broadcast_to(op(a, axis=1, keepdims=True), (qb, LANES)) for a in parts]

    def normalise(acc, ls):
        lrow = row_bcast(ls, jnp.sum)
        den = jnp.concatenate([jnp.where(lane < hd, lrow[2 * c], lrow[2 * c + 1]) for c in range(rep // 2)],
                              axis=1)
        return acc / den

    tpq = qb // LANES
    n_w = WINDOW // LANES + tpq
    w_first = i * tpq - WINDOW // LANES

    def win_mask(w):
        kpos = (w_first + w) * LANES + lane
        return (kpos >= 0) & (kpos <= t_q) & (t_q - kpos < WINDOW)

    w_off = [pl.multiple_of(jnp.maximum(w_first + w, 0) * LANES, LANES) for w in range(n_w)]
    sm_w = []
    for w in range(n_w):
        s = scores(w, w_off[w], hd)
        mk = win_mask(w)
        sm_w.append([jnp.where(mk, seg(s, r), NEG) for r in range(rep)])
    mx = sm_w[0]
    for w in range(1, n_w):
        mx = [jnp.maximum(mx[r], sm_w[w][r]) for r in range(rep)]
    mrow = row_bcast(mx, jnp.max)
    ls = [jnp.zeros((qb, LANES), F32) for _ in range(rep)]
    acc = jnp.zeros((qb, rep * hd), F32)
    for w in range(n_w):
        ps = [jnp.exp2(sm_w[w][r] - mrow[r]) for r in range(rep)]
        ls = [ls[r] + ps[r] for r in range(rep)]
        acc = acc + weighted_values(w, w_off[w], [p.astype(BF16) for p in ps], v2s_ref, v2_ref)
    o_w = normalise(acc, ls)

    n_it = lax.div((i + 1) * tpq + ATTN_UNROLL - 1, ATTN_UNROLL)

    def tile(p, u):
        kt = p * ATTN_UNROLL + u
        return kt, pl.multiple_of(jnp.minimum(kt, n_tiles - 1) * LANES, LANES)

    def pass1(p, mx):
        mx = list(mx)
        for u in range(ATTN_UNROLL):
            kt, off = tile(p, u)
            picked = _dot(sel, esel_ref[0:n_sb, pl.ds(off, LANES)])
            mk = (picked > 0.5) & (kt * LANES + lane <= t_q)
            s = scores(u, off, 0)
            sm = [jnp.where(mk, seg(s, r), NEG) for r in range(rep)]
            sc_ref[kt] = jnp.concatenate(sm, axis=1)
            mx = [jnp.maximum(mx[r], sm[r]) for r in range(rep)]
        return tuple(mx)

    mx = lax.fori_loop(0, n_it, pass1, tuple(jnp.full((qb, LANES), NEG, F32) for _ in range(rep)))
    mrow = row_bcast(mx, jnp.max)

    def pass2(p, carry):
        ls, acc = carry
        ls = list(ls)
        for u in range(ATTN_UNROLL):
            kt, off = tile(p, u)
            ps = []
            for r in range(rep):
                e = jnp.exp2(sc_ref[kt, :, LANES * r:LANES * (r + 1)] - mrow[r])
                ls[r] = ls[r] + e
                ps.append(e.astype(BF16))
            acc = acc + weighted_values(u, off, ps, v2_ref, v2s_ref)
        return tuple(ls), acc

    ls, acc = lax.fori_loop(0, n_it, pass2, (tuple(jnp.zeros((qb, LANES), F32) for _ in range(rep)),
                                             jnp.zeros((qb, rep * hd), F32)))
    o_s = normalise(acc, ls)

    gw = rep * hd
    g_hi = gate_ref[0].astype(BF16)
    g_lo = (gate_ref[0] - g_hi.astype(F32)).astype(BF16)
    gx = _dot(g_hi, eg_ref[0]) + _dot(g_lo, eg_ref[0])
    o_ref[0] = (gx[:, :gw] * o_c + gx[:, gw:2 * gw] * o_s + gx[:, 2 * gw:] * o_w).astype(o_ref.dtype)


def _attn_call(q, gates, kct, vc, kt, v2, v2s, overlap_t, esel, expand, n_sb):
    b, t, dq = q.shape
    g = KV_HEADS
    gw = dq // g
    rep = N_HEADS // KV_HEADS
    c = kct.shape[-1]
    qb = min(ATTN_Q_ROWS, t)
    assert qb % LANES == 0 and t % qb == 0 and rep % 2 == 0 and gw == 2 * LANES
    blk_q = pl.BlockSpec((1, qb, gw), lambda bi, gi, i: (bi, i, gi))
    blk_v = pl.BlockSpec((1, t, LANES), lambda bi, gi, i: (bi, 0, gi))
    fix2 = lambda bi, gi, i: (0, 0)
    n_slots = t // LANES + ATTN_UNROLL
    n_bd = max(ATTN_UNROLL, WINDOW // LANES + qb // LANES)
    return pl.pallas_call(
        functools.partial(_attn_kernel, n_sb=n_sb, top_n=min(SEL_TOPN, n_sb)),
        grid=(b, g, t // qb),
        in_specs=[blk_q,
                  pl.BlockSpec((1, qb, LANES), lambda bi, gi, i: (bi, i, 0)),
                  pl.BlockSpec((1, 1, LANES, c), lambda bi, gi, i: (bi, gi, 0, 0)),
                  pl.BlockSpec((1, 1, c, LANES), lambda bi, gi, i: (bi, gi, 0, 0)),
                  pl.BlockSpec((1, LANES, t), lambda bi, gi, i: (bi, gi, 0)),
                  blk_v, blk_v,
                  pl.BlockSpec(overlap_t.shape, fix2), pl.BlockSpec(esel.shape, fix2),
                  pl.BlockSpec((1,) + expand.shape[1:], lambda bi, gi, i: (gi, 0, 0))],
        out_specs=blk_q,
        out_shape=jax.ShapeDtypeStruct((b, t, dq), BF16),
        scratch_shapes=[pltpu.VMEM((rep * HEAD_DIM, rep * c), BF16),
                        pltpu.VMEM((rep * c, rep * HEAD_DIM), BF16),
                        pltpu.VMEM((n_bd, rep * HEAD_DIM, rep * LANES), BF16),
                        pltpu.VMEM((n_bd, rep * LANES, rep * HEAD_DIM), BF16),
                        pltpu.VMEM((n_slots, qb, rep * LANES), F32)],
        compiler_params=_cparams(3),
        name="nsa_attention",
    )(q, gates, kct, vc, kt, v2, v2s, overlap_t, esel, expand)


def _oproj_kernel(x_ref, o_ref, w_ref, gpost_ref, out_ref):
    out_ref[...] = x_ref[...] + _rms(_dot(o_ref[...], w_ref[...]), gpost_ref[...])


def _oproj_call(x2d, o, w_o, gpost, tm):
    n, d = x2d.shape
    dq = o.shape[-1]
    row = lambda i: (i, 0)
    fix = lambda i: (0, 0)
    return pl.pallas_call(
        _oproj_kernel,
        grid=(n // tm,),
        in_specs=[pl.BlockSpec((tm, d), row), pl.BlockSpec((tm, dq), row),
                  pl.BlockSpec(w_o.shape, fix), pl.BlockSpec((1, d), fix)],
        out_specs=pl.BlockSpec((tm, d), row),
        out_shape=jax.ShapeDtypeStruct((n, d), F32),
        compiler_params=_cparams(1),
        name="nsa_out_proj",
    )(x2d, o, w_o, gpost.reshape(1, d))


def _shared_kv(x, kv_norm_g, w_kv, pe_k, pe_v, k_w1, k_w2, v_w1, v_w2):
    b, t, d = x.shape
    n = b * t
    g, hd = KV_HEADS, HEAD_DIM
    tm = _row_tile(t, 512)
    def cols(ta, tb):
        return np.concatenate([np.concatenate([np.arange(hd) + ta * g * hd + gi * hd,
                                               np.arange(hd) + tb * g * hd + gi * hd]) for gi in range(g)])
    k_cols = cols(2, 4)
    wkt = jnp.concatenate([w_kv[:, k_cols], w_kv[:, k_cols[_swap_perm(k_cols.size)]]], axis=1).T.astype(BF16)
    w_rest = jnp.concatenate([w_kv[:, cols(3, 5)], w_kv[:, cols(5, 3)], w_kv[:, cols(0, 1)]], axis=1).astype(BF16)
    cos, sin = _rope_tables(jnp.arange(t), LANES)
    kt, v2, v2s, ctok = _kv_call(x.reshape(n, d), kv_norm_g, wkt, w_rest, cos.T, sin.T, b, t, tm)
    v2 = v2.reshape(b, t, g * 2 * hd)
    v2s = v2s.reshape(b, t, g * 2 * hd)
    c = t // CMP_STRIDE
    n_cmp = c - CMP_BLOCK // CMP_STRIDE + 1
    assert CMP_BLOCK == 2 * CMP_STRIDE
    st, hid = CMP_STRIDE, k_w1.shape[-1]
    pe = jnp.stack([pe_k.reshape(2, st, hd), pe_v.reshape(2, st, hd)], axis=2).reshape(2, st * 2 * hd).astype(F32)
    zw = jnp.zeros((2, st, hd, hid), k_w1.dtype)
    w1 = jnp.stack([jnp.concatenate([k_w1.reshape(2, st, hd, hid), zw], axis=-1),
                    jnp.concatenate([zw, v_w1.reshape(2, st, hd, hid)], axis=-1)], axis=2)
    w1 = w1.reshape(2, st * 2 * hd, 2 * hid).astype(BF16)
    zpad = jnp.zeros_like(k_w2)
    w2kt = jnp.stack([jnp.concatenate([k_w2, zpad], axis=1).T,
                      jnp.concatenate([k_w2[:, _swap_perm(hd)], zpad], axis=1).T]).astype(BF16)
    w2v = jnp.concatenate([v_w2, v_w2], axis=1).astype(BF16)
    pos_c = jnp.arange(c) * CMP_STRIDE + CMP_BLOCK - 1
    cos_c, sin_c = _rope_tables(pos_c, hd)
    zlane = jnp.zeros((c, LANES - hd), F32)
    cos_c = jnp.concatenate([cos_c, zlane], axis=1).T
    sin_c = jnp.concatenate([sin_c, zlane], axis=1).T
    kct, vc = _cmp_call(ctok.reshape(b, t, g * 2 * hd), pe, w1, w2kt, w2v, cos_c, sin_c, n_cmp)
    return kct, vc, kt, v2, v2s, n_cmp


def _nsa_layer(x, gpre, gpost, w_q, w_o, kct, vc, kt, v2, v2s, n_cmp):
    b, t, d = x.shape
    n = b * t
    hd = HEAD_DIM
    dq = N_HEADS * hd
    tm = _row_tile(t, 512)
    wg = jnp.zeros((d, LANES), w_q.dtype).at[:, :3 * N_HEADS].set(w_q[:, dq:])
    w_all = jnp.concatenate([w_q[:, :dq], w_q[:, :dq][:, _swap_perm(dq)], wg], axis=1).astype(BF16)
    cos, sin = _rope_tables(jnp.arange(t), LANES)
    q, gates = _q_call(x.reshape(n, d), gpre, w_all, cos, sin, t, tm)
    n_sb = t // SEL_BLOCK
    assert n_sb % 8 == 0 and n_sb <= LANES
    c = vc.shape[2]
    ci = np.arange(c)[None, :]
    sj = np.arange(LANES)[:, None]
    overlap_t = ((ci * CMP_STRIDE < (sj + 1) * SEL_BLOCK) & (ci * CMP_STRIDE + CMP_BLOCK > sj * SEL_BLOCK)
                 & (ci < n_cmp) & (sj < n_sb)).astype(np.float32)
    esel = (np.arange(LANES)[:, None] == (np.arange(t)[None, :] // SEL_BLOCK)).astype(np.float32)
    gw = dq // KV_HEADS
    expand = np.zeros((KV_HEADS, LANES, 3 * gw), np.float32)
    for gi in range(KV_HEADS):
        hh = gi * (N_HEADS // KV_HEADS) + np.arange(gw) // hd
        for j in range(3):
            expand[gi, hh * 3 + j, j * gw + np.arange(gw)] = 1.0
    o = _attn_call(q.reshape(b, t, dq), gates.reshape(b, t, LANES), kct, vc, kt, v2, v2s,
                   jnp.asarray(overlap_t, dtype=BF16), jnp.asarray(esel, dtype=BF16),
                   jnp.asarray(expand, dtype=BF16), n_sb)
    out = _oproj_call(x.reshape(n, d), o.reshape(n, dq), w_o.astype(BF16), gpost, tm)
    return out.reshape(b, t, d)


def kernel(x, a_lam_re, a_lam_im, a_log_dt, a_b_re, a_b_im, a_c_re, a_c_im, a_d, a_w_glu, b_w_q, b_w_o, kv_norm_g, w_kv, cmp_pe_k, cmp_pe_v, cmp_k_w1, cmp_k_w2, cmp_v_w1, cmp_v_w2, mix_pre_g, mix_post_g, ffn_pre_g, ffn_post_g, ffn_w_in, ffn_conv_w, ffn_conv_b, ffn_w_out):
    depth = mix_pre_g.shape[0]
    n_a = depth // 2
    kv = None
    for layer in range(depth):
        if layer < n_a:
            i = layer
            x = _s5_layer(x, mix_pre_g[layer], mix_post_g[layer], a_lam_re[i], a_lam_im[i], a_log_dt[i],
                          a_b_re[i], a_b_im[i], a_c_re[i], a_c_im[i], a_d[i], a_w_glu[i])
        else:
            j = layer - n_a
            x = _nsa_layer(x, mix_pre_g[layer], mix_post_g[layer], b_w_q[j], b_w_o[j], *kv)
        x = _ffn_layer(x, ffn_pre_g[layer], ffn_post_g[layer], ffn_w_in[layer], ffn_conv_w[layer],
                       ffn_conv_b[layer], ffn_w_out[layer])
        if layer == n_a - 1:
            kv = _shared_kv(x, kv_norm_g, w_kv, cmp_pe_k, cmp_pe_v, cmp_k_w1, cmp_k_w2, cmp_v_w1, cmp_v_w2)
    return x
```

```python
import functools
import math

import numpy as np
import jax
import jax.numpy as jnp
from jax import lax
from jax.experimental import pallas as pl
from jax.experimental.pallas import tpu as pltpu

F32 = jnp.float32
BF16 = jnp.bfloat16

S5_GROUP = 16
S5_STATE = 64
N_HEADS = 16
KV_HEADS = 4
HEAD_DIM = 64
CMP_BLOCK = 32
CMP_STRIDE = 16
SEL_BLOCK = 64
SEL_TOPN = 16
WINDOW = 512
ROPE_THETA = 500000.0
ROPE_DIM = HEAD_DIM // 4
CONV_WIDTH = 3
EPS = 1e-6
NEG = -1e30
SEL_FORCE = 1e4
SEL_NEG = -1e4
LOG2E = math.log2(math.e)

LANES = 128
S5_CHUNK = 16
CONV_HALO = 16
FFN_ROWS = 1024
FFN_HIDDEN_CAP = 1408
VMEM_LIMIT = 48 * 1024 * 1024


def _cparams(n_axes):
    return pltpu.CompilerParams(dimension_semantics=("arbitrary",) * n_axes,
                                vmem_limit_bytes=VMEM_LIMIT)


def _rms(x, g):
    return x * lax.rsqrt(jnp.mean(x * x, axis=-1, keepdims=True) + EPS) * g


def _gelu(x):
    return jax.nn.gelu(x, approximate=True)


def _dot(a, b):
    return jnp.dot(a, b, preferred_element_type=F32)


def _dot_nt(a, b):
    return lax.dot_general(a, b, (((1,), (1,)), ((), ())), preferred_element_type=F32)


def _row_tile(n, want):
    t = min(n, want)
    assert n % t == 0
    return t


def _s5_tables(lam_re, lam_im, log_dt, b_re, b_im, c_re, c_im, n_chunks):
    hp = lax.Precision.HIGHEST
    L = S5_CHUNK
    G, P = lam_re.shape
    I = b_re.shape[-1]
    dt = jnp.exp(log_dt.astype(F32))[:, None]
    lr, li = lam_re.astype(F32), lam_im.astype(F32)
    mag = jnp.exp(lr * dt)
    ab_re, ab_im = mag * jnp.cos(li * dt), mag * jnp.sin(li * dt)
    nr, ni = ab_re - 1.0, ab_im
    den = lr * lr + li * li
    coef_re = (nr * lr + ni * li) / den
    coef_im = (ni * lr - nr * li) / den
    br, bi = b_re.astype(F32), b_im.astype(F32)
    bb_re = coef_re[..., None] * br - coef_im[..., None] * bi
    bb_im = coef_re[..., None] * bi + coef_im[..., None] * br
    pr = [jnp.ones_like(ab_re)]
    pi = [jnp.zeros_like(ab_re)]
    for _ in range(L):
        r, i = pr[-1], pi[-1]
        pr.append(r * ab_re - i * ab_im)
        pi.append(r * ab_im + i * ab_re)
    pw_re = jnp.stack(pr)
    pw_im = jnp.stack(pi)
    cr, ci = c_re.astype(F32), c_im.astype(F32)
    cl_re = cr[None] * pw_re[:, :, None, :] - ci[None] * pw_im[:, :, None, :]
    cl_im = cr[None] * pw_im[:, :, None, :] + ci[None] * pw_re[:, :, None, :]
    kk = jnp.einsum('kgop,gpi->gkio', jnp.concatenate([cl_re[:L], -cl_im[:L]], axis=-1),
                    jnp.concatenate([bb_re, bb_im], axis=1), precision=hp)
    rev_re = pw_re[L - 1 - np.arange(L)]
    rev_im = pw_im[L - 1 - np.arange(L)]
    pb_re = rev_re[..., None] * bb_re[None] - rev_im[..., None] * bb_im[None]
    pb_im = rev_re[..., None] * bb_im[None] + rev_im[..., None] * bb_re[None]
    gl = LANES // I
    nt = G // gl
    ka = kk.reshape(nt, gl, L, I, I).transpose(0, 2, 1, 3, 4).reshape(nt, L * LANES, I).astype(BF16)
    ps = jnp.concatenate([pb_re, pb_im], axis=2).reshape(L, nt, gl, 2 * P, I)
    ps = ps.transpose(1, 0, 2, 4, 3).reshape(nt, L * LANES, 2 * P).astype(BF16)
    qt = jnp.concatenate([cl_re[1:], -cl_im[1:]], axis=-1).reshape(L, nt, gl, I, 2 * P)
    qt = qt.transpose(1, 0, 2, 3, 4).reshape(nt, L * LANES, 2 * P).astype(BF16)
    n_steps = int(math.ceil(math.log2(n_chunks))) if n_chunks > 1 else 0
    mr, mi = pw_re[L], pw_im[L]
    a1, a2 = [], []
    for _ in range(max(n_steps, 1)):
        a1.append(jnp.concatenate([mr, mr], axis=-1).reshape(nt, gl * 2 * P))
        a2.append(jnp.concatenate([-mi, mi], axis=-1).reshape(nt, gl * 2 * P))
        mr, mi = mr * mr - mi * mi, 2.0 * mr * mi
    pad = (-len(a1)) % 8
    a1 = jnp.stack(a1 + [jnp.zeros_like(a1[0])] * pad, axis=1)
    a2 = jnp.stack(a2 + [jnp.zeros_like(a2[0])] * pad, axis=1)
    return ka, ps, qt, a1, a2, n_steps


def _norm_kernel(x_ref, g_ref, o_ref):
    o_ref[...] = _rms(x_ref[...], g_ref[...]).astype(o_ref.dtype)


def _norm_call(x2d, g, out_dtype, tm):
    n, d = x2d.shape
    return pl.pallas_call(
        _norm_kernel,
        grid=(n // tm,),
        in_specs=[pl.BlockSpec((tm, d), lambda i: (i, 0)),
                  pl.BlockSpec((1, d), lambda i: (0, 0))],
        out_specs=pl.BlockSpec((tm, d), lambda i: (i, 0)),
        out_shape=jax.ShapeDtypeStruct((n, d), out_dtype),
        compiler_params=_cparams(1),
        name="s5_prenorm",
    )(x2d, g.reshape(1, d))


def _s5_scan_kernel(u_ref, ka_ref, ps_ref, qt_ref, rep_ref, own_ref, a1_ref, a2_ref, y_ref,
                    wy_ref, ws_ref, wq_ref, *, n_steps):
    L = S5_CHUNK
    I = S5_GROUP
    gl = LANES // I
    n_chunks = u_ref.shape[1] // L

    @pl.when(pl.program_id(1) == 0)
    def _():
        lag = _dot(ka_ref[0], rep_ref[...])
        own = own_ref[...]
        lag = [(lag[LANES * k:LANES * (k + 1)] * own).astype(BF16) for k in range(L)]
        wy_ref[...] = jnp.zeros_like(wy_ref)
        ws_ref[...] = jnp.zeros_like(ws_ref)
        wq_ref[...] = jnp.zeros_like(wq_ref)
        for s in range(L):
            for r in range(s, L):
                wy_ref[LANES * s:LANES * (s + 1), LANES * r:LANES * (r + 1)] = lag[r - s]
            for g in range(gl):
                rows = slice(LANES * s + I * g, LANES * s + I * (g + 1))
                ws_ref[rows, LANES * g:LANES * (g + 1)] = ps_ref[0, rows, :]
                wq_ref[rows, LANES * g:LANES * (g + 1)] = qt_ref[0, rows, :]

    ucat = jnp.concatenate([u_ref[0, pl.ds(s, n_chunks, stride=L), :] for s in range(L)], axis=1).astype(BF16)
    y_intra = _dot(ucat, wy_ref[...])
    ends = _dot(ucat, ws_ref[...])
    row = lax.broadcasted_iota(jnp.int32, (n_chunks, LANES), 0)
    xprev = []
    for g in range(gl):
        x = ends[:, LANES * g:LANES * (g + 1)]
        for j in range(n_steps):
            k = 1 << j
            sh = jnp.where(row >= k, pltpu.roll(x, k, 0), 0.0)
            x = (x + a1_ref[0, j:j + 1, LANES * g:LANES * (g + 1)] * sh
                 + a2_ref[0, j:j + 1, LANES * g:LANES * (g + 1)] * pltpu.roll(sh, S5_STATE, 1))
        xprev.append(jnp.where(row >= 1, pltpu.roll(x, 1, 0), 0.0).astype(BF16))
    y = y_intra + _dot_nt(jnp.concatenate(xprev, axis=1), wq_ref[...])
    for s in range(L):
        y_ref[0, pl.ds(s, n_chunks, stride=L), :] = y[:, LANES * s:LANES * (s + 1)]


def _s5_scan_call(u, ka, ps, qt, a1, a2, n_steps):
    b, t, d = u.shape
    nt = d // LANES
    gl = LANES // S5_GROUP
    wide = S5_CHUNK * LANES
    lane = np.arange(LANES)
    rep = jnp.asarray(lane[None, :] % S5_GROUP == np.arange(S5_GROUP)[:, None], BF16)
    own = jnp.asarray(lane[:, None] // S5_GROUP == lane[None, :] // S5_GROUP, F32)
    per_tile = lambda j, i: (j, 0, 0)
    fix = lambda j, i: (0, 0)
    return pl.pallas_call(
        functools.partial(_s5_scan_kernel, n_steps=n_steps),
        grid=(nt, b),
        in_specs=[pl.BlockSpec((1, t, LANES), lambda j, i: (i, 0, j)),
                  pl.BlockSpec((1,) + ka.shape[1:], per_tile),
                  pl.BlockSpec((1,) + ps.shape[1:], per_tile),
                  pl.BlockSpec((1,) + qt.shape[1:], per_tile),
                  pl.BlockSpec(rep.shape, fix), pl.BlockSpec(own.shape, fix),
                  pl.BlockSpec((1,) + a1.shape[1:], per_tile),
                  pl.BlockSpec((1,) + a2.shape[1:], per_tile)],
        out_specs=pl.BlockSpec((1, t, LANES), lambda j, i: (i, 0, j)),
        out_shape=jax.ShapeDtypeStruct((b, t, d), F32),
        scratch_shapes=[pltpu.VMEM((wide, wide), BF16),
                        pltpu.VMEM((wide, gl * 2 * S5_STATE), BF16),
                        pltpu.VMEM((wide, gl * 2 * S5_STATE), BF16)],
        compiler_params=_cparams(2),
        name="s5_scan",
    )(u, ka, ps, qt, rep, own, a1, a2)


def _s5_out_kernel(x_ref, y_ref, gpre_ref, d_ref, w_ref, gpost_ref, o_ref):
    x = x_ref[...]
    u = _rms(x, gpre_ref[...])
    z = _gelu(y_ref[...] + u * d_ref[...]).astype(BF16)
    ag = _dot(z, w_ref[...])
    d = x.shape[-1]
    m = ag[:, :d] * jax.nn.sigmoid(ag[:, d:])
    o_ref[...] = x + _rms(m, gpost_ref[...])


def _s5_out_call(x2d, y2d, gpre, dskip, wglu, gpost, tm):
    n, d = x2d.shape
    row = lambda i: (i, 0)
    fix = lambda i: (0, 0)
    return pl.pallas_call(
        _s5_out_kernel,
        grid=(n // tm,),
        in_specs=[pl.BlockSpec((tm, d), row), pl.BlockSpec((tm, d), row),
                  pl.BlockSpec((1, d), fix), pl.BlockSpec((1, d), fix),
                  pl.BlockSpec(wglu.shape, fix), pl.BlockSpec((1, d), fix)],
        out_specs=pl.BlockSpec((tm, d), row),
        out_shape=jax.ShapeDtypeStruct((n, d), F32),
        compiler_params=_cparams(1),
        name="s5_glu_out",
    )(x2d, y2d, gpre.reshape(1, d), dskip.reshape(1, d), wglu, gpost.reshape(1, d))


def _s5_layer(x, gpre, gpost, lam_re, lam_im, log_dt, b_re, b_im, c_re, c_im, d_skip, w_glu):
    b, t, d = x.shape
    n = b * t
    tm = _row_tile(n, 512)
    assert t % (8 * S5_CHUNK) == 0 and d % LANES == 0 and LANES % S5_GROUP == 0
    ka, ps, qt, a1, a2, n_steps = _s5_tables(lam_re, lam_im, log_dt, b_re, b_im, c_re, c_im, t // S5_CHUNK)
    x2d = x.reshape(n, d)
    u = _norm_call(x2d, gpre, F32, tm)
    y = _s5_scan_call(u.reshape(b, t, d), ka, ps, qt, a1, a2, n_steps)
    out = _s5_out_call(x2d, y.reshape(n, d), gpre, d_skip, w_glu.astype(BF16), gpost, tm)
    return out.reshape(b, t, d)


def _ffn_kernel(xprev_ref, x_ref, gpre_ref, wg_ref, wv_ref, cw_ref, cb_ref, wo_ref, gpost_ref,
                o_ref, xn_ref, acc_ref, *, seq_tiles):
    i = pl.program_id(0)
    j = pl.program_id(1)
    h = CONV_HALO

    @pl.when(j == 0)
    def _():
        g = gpre_ref[...]
        xn_ref[h:, :] = _rms(x_ref[...], g).astype(BF16)
        keep = jnp.where(i % seq_tiles == 0, 0.0, 1.0)
        xn_ref[:h, :] = (_rms(xprev_ref[...], g) * keep).astype(BF16)
        acc_ref[...] = jnp.zeros_like(acc_ref)

    xn = xn_ref[...]
    gate = _dot(xn, wg_ref[...])
    val = _dot(xn[h:], wv_ref[...])
    cw = cw_ref[...]
    conv = (cw[0:1] * pltpu.roll(gate, 2, 0) + cw[1:2] * pltpu.roll(gate, 1, 0)
            + cw[2:3] * gate + cb_ref[...])
    act = _gelu(conv[h:]) * val
    acc_ref[...] += _dot(act.astype(BF16), wo_ref[...])

    @pl.when(j == pl.num_programs(1) - 1)
    def _():
        o_ref[...] = x_ref[...] + _rms(acc_ref[...], gpost_ref[...])


def _ffn_layer(x, gpre, gpost, w_in, conv_w, conv_b, w_out):
    b, t, d = x.shape
    f = w_out.shape[0]
    n = b * t
    tm = _row_tile(t, FFN_ROWS)
    tf = max(w for w in range(LANES, min(f, FFN_HIDDEN_CAP) + 1, LANES) if f % w == 0)
    assert f % tf == 0 and tm % CONV_HALO == 0
    nf = f // tf
    h = CONV_HALO
    x2d = x.reshape(n, d)
    cw = jnp.zeros((8, f), F32).at[:CONV_WIDTH].set(conv_w.astype(F32))
    w_in16 = w_in.astype(BF16)
    out = pl.pallas_call(
        functools.partial(_ffn_kernel, seq_tiles=t // tm),
        grid=(n // tm, nf),
        in_specs=[pl.BlockSpec((h, d), lambda i, j: (jnp.maximum(i * (tm // h) - 1, 0), 0)),
                  pl.BlockSpec((tm, d), lambda i, j: (i, 0)),
                  pl.BlockSpec((1, d), lambda i, j: (0, 0)),
                  pl.BlockSpec((d, tf), lambda i, j: (0, j)),
                  pl.BlockSpec((d, tf), lambda i, j: (0, nf + j)),
                  pl.BlockSpec((8, tf), lambda i, j: (0, j)),
                  pl.BlockSpec((1, tf), lambda i, j: (0, j)),
                  pl.BlockSpec((tf, d), lambda i, j: (j, 0)),
                  pl.BlockSpec((1, d), lambda i, j: (0, 0))],
        out_specs=pl.BlockSpec((tm, d), lambda i, j: (i, 0)),
        out_shape=jax.ShapeDtypeStruct((n, d), F32),
        scratch_shapes=[pltpu.VMEM((tm + h, d), BF16), pltpu.VMEM((tm, d), F32)],
        compiler_params=_cparams(2),
        name="conv_ffn",
    )(x2d, x2d, gpre.reshape(1, d), w_in16, w_in16, cw, conv_b.reshape(1, f).astype(F32),
      w_out.astype(BF16), gpost.reshape(1, d))
    return out.reshape(b, t, d)


def _rope_tables(pos, width):
    half = ROPE_DIM // 2
    inv = ROPE_THETA ** (-jnp.arange(half, dtype=F32) / half)
    ang = pos.astype(F32)[:, None] * inv[None, :]
    cos, sin = jnp.cos(ang), jnp.sin(ang)
    rest = HEAD_DIM - ROPE_DIM
    n = pos.shape[0]
    c = jnp.concatenate([cos, cos, jnp.ones((n, rest), F32)], axis=-1)
    s = jnp.concatenate([-sin, sin, jnp.zeros((n, rest), F32)], axis=-1)
    reps = width // HEAD_DIM
    return jnp.tile(c, (1, reps)), jnp.tile(s, (1, reps))


def _swap_perm(width):
    half = ROPE_DIM // 2
    d = np.arange(width)
    dd = d % HEAD_DIM
    return np.where(dd < half, d + half, np.where(dd < ROPE_DIM, d - half, d))


def _kv_kernel(x_ref, g_ref, wkt_ref, w_ref, cost_ref, sint_ref, kt_ref, v_ref, vs_ref, c_ref):
    wk = v_ref.shape[-1]
    sn = _rms(x_ref[...], g_ref[...]).astype(BF16)
    rt = _dot_nt(wkt_ref[...], sn)
    reps = wk // LANES
    cos = jnp.concatenate([cost_ref[...]] * reps, axis=0)
    sin = jnp.concatenate([sint_ref[...]] * reps, axis=0)
    kt_ref[0] = (rt[:wk] * cos + rt[wk:] * sin).astype(kt_ref.dtype)
    r = _dot(sn, w_ref[...])
    v_ref[...] = r[:, :wk].astype(v_ref.dtype)
    vs_ref[...] = r[:, wk:2 * wk].astype(vs_ref.dtype)
    c_ref[...] = r[:, 2 * wk:]


def _kv_call(x2d, g, wkt, w_rest, cos_t, sin_t, b, t, tm):
    n, d = x2d.shape
    wk = KV_HEADS * 2 * HEAD_DIM
    tpt = t // tm
    row = lambda i: (i, 0)
    fix = lambda i: (0, 0)
    tab = lambda i: (0, i % tpt)
    return pl.pallas_call(
        _kv_kernel,
        grid=(n // tm,),
        in_specs=[pl.BlockSpec((tm, d), row), pl.BlockSpec((1, d), fix),
                  pl.BlockSpec(wkt.shape, fix), pl.BlockSpec(w_rest.shape, fix),
                  pl.BlockSpec((LANES, tm), tab), pl.BlockSpec((LANES, tm), tab)],
        out_specs=[pl.BlockSpec((1, wk, tm), lambda i: (i // tpt, 0, i % tpt)),
                   pl.BlockSpec((tm, wk), row), pl.BlockSpec((tm, wk), row), pl.BlockSpec((tm, wk), row)],
        out_shape=[jax.ShapeDtypeStruct((b, wk, t), BF16), jax.ShapeDtypeStruct((n, wk), BF16),
                   jax.ShapeDtypeStruct((n, wk), BF16), jax.ShapeDtypeStruct((n, wk), F32)],
        compiler_params=_cparams(1),
        name="nsa_kv_proj",
    )(x2d, g.reshape(1, d), wkt, w_rest, cos_t, sin_t)


def _cmp_kernel(z_ref, pe_ref, w1_ref, w2kt_ref, w2v_ref, cost_ref, sint_ref, kt_ref, v_ref, *, n_cmp):
    st = CMP_STRIDE
    c = z_ref.shape[1] // st
    hid = w1_ref.shape[-1] // 2
    z = jnp.concatenate([z_ref[0, pl.ds(l, c, stride=st), :] for l in range(st)], axis=1)
    top = _dot((z + pe_ref[0:1, :]).astype(BF16), w1_ref[0])
    bot = _dot((z + pe_ref[1:2, :]).astype(BF16), w1_ref[1])
    hdn = _gelu(top + pltpu.roll(bot, c - 1, 0)).astype(BF16)
    res = [hdn[:, :hid], hdn[:, hid:]]
    kt = (_dot_nt(w2kt_ref[0], res[0]) * cost_ref[...] + _dot_nt(w2kt_ref[1], res[0]) * sint_ref[...])
    col = lax.broadcasted_iota(jnp.int32, kt.shape, 1)
    kt_ref[0, 0] = jnp.where(col < n_cmp, kt, 0.0).astype(kt_ref.dtype)
    v = _dot(res[1], w2v_ref[...])
    row = lax.broadcasted_iota(jnp.int32, v.shape, 0)
    v_ref[0, 0] = jnp.where(row < n_cmp, v, 0.0).astype(v_ref.dtype)


def _cmp_call(ctok, pe, w1, w2kt, w2v, cos_t, sin_t, n_cmp):
    b, t, wid = ctok.shape
    g = wid // LANES
    c = t // CMP_STRIDE
    fix2 = lambda i, j: (0, 0)
    fix3 = lambda i, j: (0, 0, 0)
    return pl.pallas_call(
        functools.partial(_cmp_kernel, n_cmp=n_cmp),
        grid=(b, g),
        in_specs=[pl.BlockSpec((1, t, LANES), lambda i, j: (i, 0, j)),
                  pl.BlockSpec(pe.shape, fix2), pl.BlockSpec(w1.shape, fix3),
                  pl.BlockSpec(w2kt.shape, fix3), pl.BlockSpec(w2v.shape, fix2),
                  pl.BlockSpec(cos_t.shape, fix2), pl.BlockSpec(sin_t.shape, fix2)],
        out_specs=[pl.BlockSpec((1, 1, LANES, c), lambda i, j: (i, j, 0, 0)),
                   pl.BlockSpec((1, 1, c, LANES), lambda i, j: (i, j, 0, 0))],
        out_shape=[jax.ShapeDtypeStruct((b, g, LANES, c), BF16), jax.ShapeDtypeStruct((b, g, c, LANES), BF16)],
        compiler_params=_cparams(2),
        name="nsa_compress",
    )(ctok, pe, w1, w2kt, w2v, cos_t, sin_t)


def _q_kernel(x_ref, g_ref, w_ref, cos_ref, sin_ref, q_ref, gate_ref):
    d = q_ref.shape[-1]
    hn = _rms(x_ref[...], g_ref[...]).astype(BF16)
    r = _dot(hn, w_ref[...])
    reps = d // LANES
    cos = jnp.concatenate([cos_ref[...]] * reps, axis=1)
    sin = jnp.concatenate([sin_ref[...]] * reps, axis=1)
    q_ref[...] = ((r[:, :d] * cos + r[:, d:2 * d] * sin) * (HEAD_DIM ** -0.5 * LOG2E)).astype(q_ref.dtype)
    gate_ref[...] = jax.nn.sigmoid(r[:, 2 * d:])


def _q_call(x2d, g, w_all, cos, sin, t, tm):
    n, d = x2d.shape
    dq = N_HEADS * HEAD_DIM
    tpt = t // tm
    row = lambda i: (i, 0)
    fix = lambda i: (0, 0)
    tab = lambda i: (i % tpt, 0)
    return pl.pallas_call(
        _q_kernel,
        grid=(n // tm,),
        in_specs=[pl.BlockSpec((tm, d), row), pl.BlockSpec((1, d), fix),
                  pl.BlockSpec(w_all.shape, fix),
                  pl.BlockSpec((tm, LANES), tab), pl.BlockSpec((tm, LANES), tab)],
        out_specs=[pl.BlockSpec((tm, dq), row), pl.BlockSpec((tm, LANES), row)],
        out_shape=[jax.ShapeDtypeStruct((n, dq), BF16), jax.ShapeDtypeStruct((n, LANES), F32)],
        compiler_params=_cparams(1),
        name="nsa_q_proj",
    )(x2d, g.reshape(1, d), w_all, cos, sin)


ATTN_UNROLL = 4
ATTN_Q_ROWS = 256


def _attn_kernel(q_ref, gate_ref, kct_ref, vc_ref, kt_ref, v2_ref, v2s_ref, ovt_ref, esel_ref, eg_ref,
                 o_ref, kcbd_ref, vcbd_ref, kbd_ref, vbd_ref, sc_ref, *, n_sb, top_n):
    qb = q_ref.shape[1]
    rep = N_HEADS // KV_HEADS
    hd = HEAD_DIM
    n_c = kct_ref.shape[-1]
    n_tiles = kt_ref.shape[-1] // LANES
    i = pl.program_id(2)
    start = i * qb

    @pl.when(i == 0)
    def _():
        kcbd_ref[...] = jnp.zeros_like(kcbd_ref)
        vcbd_ref[...] = jnp.zeros_like(vcbd_ref)
        kbd_ref[...] = jnp.zeros_like(kbd_ref)
        vbd_ref[...] = jnp.zeros_like(vbd_ref)
        vc = vc_ref[0, 0]
        for r in range(rep):
            h = hd * (r % 2)
            kcbd_ref[hd * r:hd * (r + 1), n_c * r:n_c * (r + 1)] = kct_ref[0, 0, 0:hd, :]
            vcbd_ref[n_c * r:n_c * (r + 1), hd * r:hd * (r + 1)] = vc[:, h:h + hd]

    qblk = q_ref[0]

    s1 = _dot(qblk, kcbd_ref[...])
    n_idx = lax.broadcasted_iota(jnp.int32, (qb, n_c), 1)
    t1 = start + lax.broadcasted_iota(jnp.int32, (qb, n_c), 0)
    m1 = n_idx * CMP_STRIDE + (CMP_BLOCK - 1) <= t1
    psum = jnp.zeros((qb, n_c), F32)
    p_parts = []
    for r in range(rep):
        sm = jnp.where(m1, s1[:, n_c * r:n_c * (r + 1)], NEG)
        mrow = jnp.max(sm, axis=1, keepdims=True)
        e = jnp.exp2(sm - jnp.where(mrow > 0.5 * NEG, mrow, 0.0))
        l = jnp.sum(e, axis=1, keepdims=True)
        p = e / jnp.where(l > 0.0, l, 1.0)
        psum = psum + p
        p_parts.append(p.astype(BF16))
    o_c = _dot(jnp.concatenate(p_parts, axis=1), vcbd_ref[...])

    p_hi = psum.astype(BF16)
    p_lo = (psum - p_hi.astype(F32)).astype(BF16)
    imp_t = _dot_nt(ovt_ref[...], p_hi) + _dot_nt(ovt_ref[...], p_lo)
    blk = lax.broadcasted_iota(jnp.int32, (n_sb, qb), 0)
    t_l = start + lax.broadcasted_iota(jnp.int32, (n_sb, qb), 1)
    cur = jnp.right_shift(t_l, int(math.log2(SEL_BLOCK)))
    forced = (blk == 0) | (blk == cur) | (blk == cur - 1)
    score = jnp.where(forced, SEL_FORCE, jnp.where(blk <= cur, imp_t[:n_sb], SEL_NEG))
    blk_f = blk.astype(F32)
    work = score
    sel_t = jnp.zeros((n_sb, qb), F32)
    for _ in range(top_n):
        best = jnp.max(work, axis=0, keepdims=True)
        first = jnp.min(jnp.where(work == best, blk_f, float(n_sb)), axis=0, keepdims=True)
        hit = blk_f == first
        sel_t = jnp.where(hit, 1.0, sel_t)
        work = jnp.where(hit, -jnp.inf, work)
    sel_t = jnp.where(score > 0.5 * SEL_NEG, sel_t, 0.0)
    sel = sel_t.T.astype(BF16)

    lane = lax.broadcasted_iota(jnp.int32, (qb, LANES), 1)
    t_q = start + lax.broadcasted_iota(jnp.int32, (qb, LANES), 0)

    def scores(u, off, row0):
        ktile = kt_ref[0, row0:row0 + hd, pl.ds(off, LANES)]
        for r in range(rep):
            kbd_ref[u, hd * r:hd * (r + 1), LANES * r:LANES * (r + 1)] = ktile
        return _dot(qblk, kbd_ref[u])

    def weighted_values(u, off, ps, v_even, v_odd):
        ve = v_even[0, pl.ds(off, LANES), :]
        vo = v_odd[0, pl.ds(off, LANES), :]
        for r in range(rep):
            h = hd * (r % 2)
            src = ve if r % 2 == 0 else vo
            vbd_ref[u, LANES * r:LANES * (r + 1), hd * r:hd * (r + 1)] = src[:, h:h + hd]
        return _dot(jnp.concatenate(ps, axis=1), vbd_ref[u])

    def seg(a, r):
        return a[:, LANES * r:LANES * (r + 1)]

    def row_bcast(parts, op):
        return [jnp.broadcast_to(op(a, axis=1, keepdims=True), (qb, LANES)) for a in parts]

    def normalise(acc, ls):
        lrow = row_bcast(ls, jnp.sum)
        den = jnp.concatenate([jnp.where(lane < hd, lrow[2 * c], lrow[2 * c + 1]) for c in range(rep // 2)],
                              axis=1)
        return acc / den

    tpq = qb // LANES
    n_w = WINDOW // LANES + tpq
    w_first = i * tpq - WINDOW // LANES

    def win_mask(w):
        kpos = (w_first + w) * LANES + lane
        return (kpos >= 0) & (kpos <= t_q) & (t_q - kpos < WINDOW)

    w_off = [pl.multiple_of(jnp.maximum(w_first + w, 0) * LANES, LANES) for w in range(n_w)]
    mx = [jnp.full((qb, LANES), NEG, F32) for _ in range(rep)]
    for w in range(n_w):
        s = scores(w, w_off[w], hd)
        mk = win_mask(w)
        sm = [jnp.where(mk, seg(s, r), NEG) for r in range(rep)]
        sc_ref[w] = jnp.concatenate(sm, axis=1)
        mx = [jnp.maximum(mx[r], sm[r]) for r in range(rep)]
    mrow = row_bcast(mx, jnp.max)
    ls = [jnp.zeros((qb, LANES), F32) for _ in range(rep)]
    acc = jnp.zeros((qb, rep * hd), F32)
    for w in range(n_w):
        ps = [jnp.exp2(sc_ref[w, :, LANES * r:LANES * (r + 1)] - mrow[r]) for r in range(rep)]
        ls = [ls[r] + ps[r] for r in range(rep)]
        acc = acc + weighted_values(w, w_off[w], [p.astype(BF16) for p in ps], v2s_ref, v2_ref)
    o_w = normalise(acc, ls)

    gw = rep * hd
    g_hi = gate_ref[0].astype(BF16)
    g_lo = (gate_ref[0] - g_hi.astype(F32)).astype(BF16)
    gx = _dot(g_hi, eg_ref[0]) + _dot(g_lo, eg_ref[0])
    o_cw = gx[:, :gw] * o_c + gx[:, 2 * gw:] * o_w
    g_s = gx[:, gw:2 * gw]

    n_it = lax.div((i + 1) * tpq + ATTN_UNROLL - 1, ATTN_UNROLL)

    def tile(p, u):
        kt = p * ATTN_UNROLL + u
        return kt, pl.multiple_of(jnp.minimum(kt, n_tiles - 1) * LANES, LANES)

    def pass1(p, mx):
        mx = list(mx)
        for u in range(ATTN_UNROLL):
            kt, off = tile(p, u)
            picked = _dot(sel, esel_ref[0:n_sb, pl.ds(off, LANES)])
            mk = (picked > 0.5) & (kt * LANES + lane <= t_q)
            s = scores(u, off, 0)
            sm = [jnp.where(mk, seg(s, r), NEG) for r in range(rep)]
            sc_ref[kt] = jnp.concatenate(sm, axis=1)
            mx = [jnp.maximum(mx[r], sm[r]) for r in range(rep)]
        return tuple(mx)

    mx = lax.fori_loop(0, n_it, pass1, tuple(jnp.full((qb, LANES), NEG, F32) for _ in range(rep)))
    mrow = row_bcast(mx, jnp.max)

    def pass2(p, carry):
        ls, acc = carry
        ls = list(ls)
        for u in range(ATTN_UNROLL):
            kt, off = tile(p, u)
            ps = []
            for r in range(rep):
                e = jnp.exp2(sc_ref[kt, :, LANES * r:LANES * (r + 1)] - mrow[r])
                ls[r] = ls[r] + e
                ps.append(e.astype(BF16))
            acc = acc + weighted_values(u, off, ps, v2_ref, v2s_ref)
        return tuple(ls), acc

    ls, acc = lax.fori_loop(0, n_it, pass2, (tuple(jnp.zeros((qb, LANES), F32) for _ in range(rep)),
                                             jnp.zeros((qb, rep * hd), F32)))
    o_ref[0] = (o_cw + g_s * normalise(acc, ls)).astype(o_ref.dtype)


def _attn_call(q, gates, kct, vc, kt, v2, v2s, overlap_t, esel, expand, n_sb):
    b, t, dq = q.shape
    g = KV_HEADS
    gw = dq // g
    rep = N_HEADS // KV_HEADS
    c = kct.shape[-1]
    qb = min(ATTN_Q_ROWS, t)
    assert qb % LANES == 0 and t % qb == 0 and rep % 2 == 0 and gw == 2 * LANES
    blk_q = pl.BlockSpec((1, qb, gw), lambda bi, gi, i: (bi, i, gi))
    blk_v = pl.BlockSpec((1, t, LANES), lambda bi, gi, i: (bi, 0, gi))
    fix2 = lambda bi, gi, i: (0, 0)
    n_slots = t // LANES + ATTN_UNROLL
    n_bd = max(ATTN_UNROLL, WINDOW // LANES + qb // LANES)
    return pl.pallas_call(
        functools.partial(_attn_kernel, n_sb=n_sb, top_n=min(SEL_TOPN, n_sb)),
        grid=(b, g, t // qb),
        in_specs=[blk_q,
                  pl.BlockSpec((1, qb, LANES), lambda bi, gi, i: (bi, i, 0)),
                  pl.BlockSpec((1, 1, LANES, c), lambda bi, gi, i: (bi, gi, 0, 0)),
                  pl.BlockSpec((1, 1, c, LANES), lambda bi, gi, i: (bi, gi, 0, 0)),
                  pl.BlockSpec((1, LANES, t), lambda bi, gi, i: (bi, gi, 0)),
                  blk_v, blk_v,
                  pl.BlockSpec(overlap_t.shape, fix2), pl.BlockSpec(esel.shape, fix2),
                  pl.BlockSpec((1,) + expand.shape[1:], lambda bi, gi, i: (gi, 0, 0))],
        out_specs=blk_q,
        out_shape=jax.ShapeDtypeStruct((b, t, dq), BF16),
        scratch_shapes=[pltpu.VMEM((rep * HEAD_DIM, rep * c), BF16),
                        pltpu.VMEM((rep * c, rep * HEAD_DIM), BF16),
                        pltpu.VMEM((n_bd, rep * HEAD_DIM, rep * LANES), BF16),
                        pltpu.VMEM((n_bd, rep * LANES, rep * HEAD_DIM), BF16),
                        pltpu.VMEM((n_slots, qb, rep * LANES), F32)],
        compiler_params=_cparams(3),
        name="nsa_attention",
    )(q, gates, kct, vc, kt, v2, v2s, overlap_t, esel, expand)


def _oproj_kernel(x_ref, o_ref, w_ref, gpost_ref, out_ref):
    out_ref[...] = x_ref[...] + _rms(_dot(o_ref[...], w_ref[...]), gpost_ref[...])


def _oproj_call(x2d, o, w_o, gpost, tm):
    n, d = x2d.shape
    dq = o.shape[-1]
    row = lambda i: (i, 0)
    fix = lambda i: (0, 0)
    return pl.pallas_call(
        _oproj_kernel,
        grid=(n // tm,),
        in_specs=[pl.BlockSpec((tm, d), row), pl.BlockSpec((tm, dq), row),
                  pl.BlockSpec(w_o.shape, fix), pl.BlockSpec((1, d), fix)],
        out_specs=pl.BlockSpec((tm, d), row),
        out_shape=jax.ShapeDtypeStruct((n, d), F32),
        compiler_params=_cparams(1),
        name="nsa_out_proj",
    )(x2d, o, w_o, gpost.reshape(1, d))


def _shared_kv(x, kv_norm_g, w_kv, pe_k, pe_v, k_w1, k_w2, v_w1, v_w2):
    b, t, d = x.shape
    n = b * t
    g, hd = KV_HEADS, HEAD_DIM
    tm = _row_tile(t, 512)
    def cols(ta, tb):
        return np.concatenate([np.concatenate([np.arange(hd) + ta * g * hd + gi * hd,
                                               np.arange(hd) + tb * g * hd + gi * hd]) for gi in range(g)])
    k_cols = cols(2, 4)
    wkt = jnp.concatenate([w_kv[:, k_cols], w_kv[:, k_cols[_swap_perm(k_cols.size)]]], axis=1).T.astype(BF16)
    w_rest = jnp.concatenate([w_kv[:, cols(3, 5)], w_kv[:, cols(5, 3)], w_kv[:, cols(0, 1)]], axis=1).astype(BF16)
    cos, sin = _rope_tables(jnp.arange(t), LANES)
    kt, v2, v2s, ctok = _kv_call(x.reshape(n, d), kv_norm_g, wkt, w_rest, cos.T, sin.T, b, t, tm)
    v2 = v2.reshape(b, t, g * 2 * hd)
    v2s = v2s.reshape(b, t, g * 2 * hd)
    c = t // CMP_STRIDE
    n_cmp = c - CMP_BLOCK // CMP_STRIDE + 1
    assert CMP_BLOCK == 2 * CMP_STRIDE
    st, hid = CMP_STRIDE, k_w1.shape[-1]
    pe = jnp.stack([pe_k.reshape(2, st, hd), pe_v.reshape(2, st, hd)], axis=2).reshape(2, st * 2 * hd).astype(F32)
    zw = jnp.zeros((2, st, hd, hid), k_w1.dtype)
    w1 = jnp.stack([jnp.concatenate([k_w1.reshape(2, st, hd, hid), zw], axis=-1),
                    jnp.concatenate([zw, v_w1.reshape(2, st, hd, hid)], axis=-1)], axis=2)
    w1 = w1.reshape(2, st * 2 * hd, 2 * hid).astype(BF16)
    zpad = jnp.zeros_like(k_w2)
    w2kt = jnp.stack([jnp.concatenate([k_w2, zpad], axis=1).T,
                      jnp.concatenate([k_w2[:, _swap_perm(hd)], zpad], axis=1).T]).astype(BF16)
    w2v = jnp.concatenate([v_w2, v_w2], axis=1).astype(BF16)
    pos_c = jnp.arange(c) * CMP_STRIDE + CMP_BLOCK - 1
    cos_c, sin_c = _rope_tables(pos_c, hd)
    zlane = jnp.zeros((c, LANES - hd), F32)
    cos_c = jnp.concatenate([cos_c, zlane], axis=1).T
    sin_c = jnp.concatenate([sin_c, zlane], axis=1).T
    kct, vc = _cmp_call(ctok.reshape(b, t, g * 2 * hd), pe, w1, w2kt, w2v, cos_c, sin_c, n_cmp)
    return kct, vc, kt, v2, v2s, n_cmp


def _nsa_layer(x, gpre, gpost, w_q, w_o, kct, vc, kt, v2, v2s, n_cmp):
    b, t, d = x.shape
    n = b * t
    hd = HEAD_DIM
    dq = N_HEADS * hd
    tm = _row_tile(t, 512)
    wg = jnp.zeros((d, LANES), w_q.dtype).at[:, :3 * N_HEADS].set(w_q[:, dq:])
    w_all = jnp.concatenate([w_q[:, :dq], w_q[:, :dq][:, _swap_perm(dq)], wg], axis=1).astype(BF16)
    cos, sin = _rope_tables(jnp.arange(t), LANES)
    q, gates = _q_call(x.reshape(n, d), gpre, w_all, cos, sin, t, tm)
    n_sb = t // SEL_BLOCK
    assert n_sb % 8 == 0 and n_sb <= LANES
    c = vc.shape[2]
    ci = np.arange(c)[None, :]
    sj = np.arange(LANES)[:, None]
    overlap_t = ((ci * CMP_STRIDE < (sj + 1) * SEL_BLOCK) & (ci * CMP_STRIDE + CMP_BLOCK > sj * SEL_BLOCK)
                 & (ci < n_cmp) & (sj < n_sb)).astype(np.float32)
    esel = (np.arange(LANES)[:, None] == (np.arange(t)[None, :] // SEL_BLOCK)).astype(np.float32)
    gw = dq // KV_HEADS
    expand = np.zeros((KV_HEADS, LANES, 3 * gw), np.float32)
    for gi in range(KV_HEADS):
        hh = gi * (N_HEADS // KV_HEADS) + np.arange(gw) // hd
        for j in range(3):
            expand[gi, hh * 3 + j, j * gw + np.arange(gw)] = 1.0
    o = _attn_call(q.reshape(b, t, dq), gates.reshape(b, t, LANES), kct, vc, kt, v2, v2s,
                   jnp.asarray(overlap_t, dtype=BF16), jnp.asarray(esel, dtype=BF16),
                   jnp.asarray(expand, dtype=BF16), n_sb)
    out = _oproj_call(x.reshape(n, d), o.reshape(n, dq), w_o.astype(BF16), gpost, tm)
    return out.reshape(b, t, d)


def kernel(x, a_lam_re, a_lam_im, a_log_dt, a_b_re, a_b_im, a_c_re, a_c_im, a_d, a_w_glu, b_w_q, b_w_o, kv_norm_g, w_kv, cmp_pe_k, cmp_pe_v, cmp_k_w1, cmp_k_w2, cmp_v_w1, cmp_v_w2, mix_pre_g, mix_post_g, ffn_pre_g, ffn_post_g, ffn_w_in, ffn_conv_w, ffn_conv_b, ffn_w_out):
    depth = mix_pre_g.shape[0]
    n_a = depth // 2
    kv = None
    for layer in range(depth):
        if layer < n_a:
            i = layer
            x = _s5_layer(x, mix_pre_g[layer], mix_post_g[layer], a_lam_re[i], a_lam_im[i], a_log_dt[i],
                          a_b_re[i], a_b_im[i], a_c_re[i], a_c_im[i], a_d[i], a_w_glu[i])
        else:
            j = layer - n_a
            x = _nsa_layer(x, mix_pre_g[layer], mix_post_g[layer], b_w_q[j], b_w_o[j], *kv)
        x = _ffn_layer(x, ffn_pre_g[layer], ffn_post_g[layer], ffn_w_in[layer], ffn_conv_w[layer],
                       ffn_conv_b[layer], ffn_w_out[layer])
        if layer == n_a - 1:
            kv = _shared_kv(x, kv_norm_g, w_kv, cmp_pe_k, cmp_pe_v, cmp_k_w1, cmp_k_w2, cmp_v_w1, cmp_v_w2)
    return x
```

```python
import functools
import math

import numpy as np
import jax
import jax.numpy as jnp
from jax import lax
from jax.experimental import pallas as pl
from jax.experimental.pallas import tpu as pltpu

F32 = jnp.float32
BF16 = jnp.bfloat16

S5_GROUP = 16
S5_STATE = 64
N_HEADS = 16
KV_HEADS = 4
HEAD_DIM = 64
CMP_BLOCK = 32
CMP_STRIDE = 16
SEL_BLOCK = 64
SEL_TOPN = 16
WINDOW = 512
ROPE_THETA = 500000.0
ROPE_DIM = HEAD_DIM // 4
CONV_WIDTH = 3
EPS = 1e-6
NEG = -1e30
SEL_FORCE = 1e4
SEL_NEG = -1e4
LOG2E = math.log2(math.e)

LANES = 128
MXU_TILE = 256
S5_CHUNK = 16
CONV_HALO = 16
FFN_ROWS = 1024
FFN_HIDDEN_CAP = 1408
VMEM_LIMIT = 48 * 1024 * 1024


def _cparams(n_axes):
    return pltpu.CompilerParams(dimension_semantics=("arbitrary",) * n_axes,
                                vmem_limit_bytes=VMEM_LIMIT)


def _rms(x, g):
    return x * lax.rsqrt(jnp.mean(x * x, axis=-1, keepdims=True) + EPS) * g


def _gelu(x):
    return jax.nn.gelu(x, approximate=True)


def _dot(a, b):
    return jnp.dot(a, b, preferred_element_type=F32)


def _dot_nt(a, b):
    return lax.dot_general(a, b, (((1,), (1,)), ((), ())), preferred_element_type=F32)


def _row_tile(n, want):
    t = min(n, want)
    assert n % t == 0
    return t


def _s5_tables(lam_re, lam_im, log_dt, b_re, b_im, c_re, c_im, n_chunks):
    L = S5_CHUNK
    G, P = lam_re.shape
    I = b_re.shape[-1]
    dt = jnp.exp(log_dt.astype(F32))[:, None]
    lr, li = lam_re.astype(F32), lam_im.astype(F32)
    mag = jnp.exp(lr * dt)
    ab_re, ab_im = mag * jnp.cos(li * dt), mag * jnp.sin(li * dt)
    nr, ni = ab_re - 1.0, ab_im
    den = lr * lr + li * li
    coef_re = (nr * lr + ni * li) / den
    coef_im = (ni * lr - nr * li) / den
    br, bi = b_re.astype(F32), b_im.astype(F32)
    bb_re = coef_re[..., None] * br - coef_im[..., None] * bi
    bb_im = coef_re[..., None] * bi + coef_im[..., None] * br
    pr = [jnp.ones_like(ab_re)]
    pi = [jnp.zeros_like(ab_re)]
    for _ in range(L):
        r, i = pr[-1], pi[-1]
        pr.append(r * ab_re - i * ab_im)
        pi.append(r * ab_im + i * ab_re)
    pw_re = jnp.stack(pr)
    pw_im = jnp.stack(pi)
    cr, ci = c_re.astype(F32), c_im.astype(F32)
    cl_re = cr[None] * pw_re[:, :, None, :] - ci[None] * pw_im[:, :, None, :]
    cl_im = cr[None] * pw_im[:, :, None, :] + ci[None] * pw_re[:, :, None, :]
    kk = jnp.einsum('kgop,gpi->gkio', jnp.concatenate([cl_re[:L], -cl_im[:L]], axis=-1),
                    jnp.concatenate([bb_re, bb_im], axis=1))
    rev_re = pw_re[L - 1 - np.arange(L)]
    rev_im = pw_im[L - 1 - np.arange(L)]
    pb_re = rev_re[..., None] * bb_re[None] - rev_im[..., None] * bb_im[None]
    pb_im = rev_re[..., None] * bb_im[None] + rev_im[..., None] * bb_re[None]
    gl = LANES // I
    nt = G // gl
    ka = kk.reshape(nt, gl, L, I, I).transpose(0, 2, 1, 3, 4).reshape(nt, L * LANES, I).astype(BF16)
    ps = jnp.concatenate([pb_re, pb_im], axis=2).reshape(L, nt, gl, 2 * P, I)
    ps = ps.transpose(1, 0, 2, 4, 3).reshape(nt, L * LANES, 2 * P).astype(BF16)
    qt = jnp.concatenate([cl_re[1:], -cl_im[1:]], axis=-1).reshape(L, nt, gl, I, 2 * P)
    qt = qt.transpose(1, 0, 2, 3, 4).reshape(nt, L * LANES, 2 * P).astype(BF16)
    n_steps = int(math.ceil(math.log2(n_chunks))) if n_chunks > 1 else 0
    mr, mi = pw_re[L], pw_im[L]
    a1, a2 = [], []
    for _ in range(max(n_steps, 1)):
        a1.append(jnp.concatenate([mr, mr], axis=-1).reshape(nt, gl * 2 * P))
        a2.append(jnp.concatenate([-mi, mi], axis=-1).reshape(nt, gl * 2 * P))
        mr, mi = mr * mr - mi * mi, 2.0 * mr * mi
    pad = (-len(a1)) % 8
    a1 = jnp.stack(a1 + [jnp.zeros_like(a1[0])] * pad, axis=1)
    a2 = jnp.stack(a2 + [jnp.zeros_like(a2[0])] * pad, axis=1)
    return ka, ps, qt, a1, a2, n_steps


def _norm_kernel(x_ref, g_ref, o_ref):
    o_ref[...] = _rms(x_ref[...], g_ref[...]).astype(o_ref.dtype)


def _norm_call(x2d, g, out_dtype, tm):
    n, d = x2d.shape
    return pl.pallas_call(
        _norm_kernel,
        grid=(n // tm,),
        in_specs=[pl.BlockSpec((tm, d), lambda i: (i, 0)),
                  pl.BlockSpec((1, d), lambda i: (0, 0))],
        out_specs=pl.BlockSpec((tm, d), lambda i: (i, 0)),
        out_shape=jax.ShapeDtypeStruct((n, d), out_dtype),
        compiler_params=_cparams(1),
        name="s5_prenorm",
    )(x2d, g.reshape(1, d))


def _s5_scan_kernel(u_ref, ka_ref, ps_ref, qt_ref, rep_ref, own_ref, a1_ref, a2_ref, y_ref,
                    wy_ref, ws_ref, wq_ref, *, n_steps):
    L = S5_CHUNK
    I = S5_GROUP
    gl = LANES // I
    n_chunks = u_ref.shape[1] // L

    @pl.when(pl.program_id(1) == 0)
    def _():
        lag = _dot(ka_ref[0], rep_ref[...])
        own = own_ref[...]
        lag = [(lag[LANES * k:LANES * (k + 1)] * own).astype(BF16) for k in range(L)]
        wy_ref[...] = jnp.zeros_like(wy_ref)
        ws_ref[...] = jnp.zeros_like(ws_ref)
        wq_ref[...] = jnp.zeros_like(wq_ref)
        for s in range(L):
            for r in range(s, L):
                wy_ref[LANES * s:LANES * (s + 1), LANES * r:LANES * (r + 1)] = lag[r - s]
            for g in range(gl):
                rows = slice(LANES * s + I * g, LANES * s + I * (g + 1))
                ws_ref[rows, LANES * g:LANES * (g + 1)] = ps_ref[0, rows, :]
                wq_ref[rows, LANES * g:LANES * (g + 1)] = qt_ref[0, rows, :]

    ucat = jnp.concatenate([u_ref[0, pl.ds(s, n_chunks, stride=L), :] for s in range(L)], axis=1).astype(BF16)
    y_intra = jnp.concatenate([_dot(ucat[:, :c1], wy_ref[:c1, c1 - MXU_TILE:c1])
                               for c1 in range(MXU_TILE, L * LANES + 1, MXU_TILE)], axis=1)
    ends = _dot(ucat, ws_ref[...])
    row = lax.broadcasted_iota(jnp.int32, (n_chunks, LANES), 0)
    xprev = []
    for g in range(gl):
        x = ends[:, LANES * g:LANES * (g + 1)]
        for j in range(n_steps):
            k = 1 << j
            sh = jnp.where(row >= k, pltpu.roll(x, k, 0), 0.0)
            x = (x + a1_ref[0, j:j + 1, LANES * g:LANES * (g + 1)] * sh
                 + a2_ref[0, j:j + 1, LANES * g:LANES * (g + 1)] * pltpu.roll(sh, S5_STATE, 1))
        xprev.append(jnp.where(row >= 1, pltpu.roll(x, 1, 0), 0.0).astype(BF16))
    y = y_intra + _dot_nt(jnp.concatenate(xprev, axis=1), wq_ref[...])
    for s in range(L):
        y_ref[0, pl.ds(s, n_chunks, stride=L), :] = y[:, LANES * s:LANES * (s + 1)]


def _s5_scan_call(u, ka, ps, qt, a1, a2, n_steps):
    b, t, d = u.shape
    nt = d // LANES
    gl = LANES // S5_GROUP
    wide = S5_CHUNK * LANES
    lane = np.arange(LANES)
    rep = jnp.asarray(lane[None, :] % S5_GROUP == np.arange(S5_GROUP)[:, None], BF16)
    own = jnp.asarray(lane[:, None] // S5_GROUP == lane[None, :] // S5_GROUP, F32)
    per_tile = lambda j, i: (j, 0, 0)
    fix = lambda j, i: (0, 0)
    return pl.pallas_call(
        functools.partial(_s5_scan_kernel, n_steps=n_steps),
        grid=(nt, b),
        in_specs=[pl.BlockSpec((1, t, LANES), lambda j, i: (i, 0, j)),
                  pl.BlockSpec((1,) + ka.shape[1:], per_tile),
                  pl.BlockSpec((1,) + ps.shape[1:], per_tile),
                  pl.BlockSpec((1,) + qt.shape[1:], per_tile),
                  pl.BlockSpec(rep.shape, fix), pl.BlockSpec(own.shape, fix),
                  pl.BlockSpec((1,) + a1.shape[1:], per_tile),
                  pl.BlockSpec((1,) + a2.shape[1:], per_tile)],
        out_specs=pl.BlockSpec((1, t, LANES), lambda j, i: (i, 0, j)),
        out_shape=jax.ShapeDtypeStruct((b, t, d), F32),
        scratch_shapes=[pltpu.VMEM((wide, wide), BF16),
                        pltpu.VMEM((wide, gl * 2 * S5_STATE), BF16),
                        pltpu.VMEM((wide, gl * 2 * S5_STATE), BF16)],
        compiler_params=_cparams(2),
        name="s5_scan",
    )(u, ka, ps, qt, rep, own, a1, a2)


def _s5_out_kernel(x_ref, y_ref, gpre_ref, d_ref, w_ref, gpost_ref, o_ref):
    x = x_ref[...]
    u = _rms(x, gpre_ref[...])
    z = _gelu(y_ref[...] + u * d_ref[...]).astype(BF16)
    ag = _dot(z, w_ref[...])
    d = x.shape[-1]
    m = ag[:, :d] * jax.nn.sigmoid(ag[:, d:])
    o_ref[...] = x + _rms(m, gpost_ref[...])


def _s5_out_call(x2d, y2d, gpre, dskip, wglu, gpost, tm):
    n, d = x2d.shape
    row = lambda i: (i, 0)
    fix = lambda i: (0, 0)
    return pl.pallas_call(
        _s5_out_kernel,
        grid=(n // tm,),
        in_specs=[pl.BlockSpec((tm, d), row), pl.BlockSpec((tm, d), row),
                  pl.BlockSpec((1, d), fix), pl.BlockSpec((1, d), fix),
                  pl.BlockSpec(wglu.shape, fix), pl.BlockSpec((1, d), fix)],
        out_specs=pl.BlockSpec((tm, d), row),
        out_shape=jax.ShapeDtypeStruct((n, d), F32),
        compiler_params=_cparams(1),
        name="s5_glu_out",
    )(x2d, y2d, gpre.reshape(1, d), dskip.reshape(1, d), wglu, gpost.reshape(1, d))


def _s5_layer(x, gpre, gpost, lam_re, lam_im, log_dt, b_re, b_im, c_re, c_im, d_skip, w_glu):
    b, t, d = x.shape
    n = b * t
    tm = _row_tile(n, 512)
    assert t % (8 * S5_CHUNK) == 0 and d % LANES == 0 and LANES % S5_GROUP == 0
    ka, ps, qt, a1, a2, n_steps = _s5_tables(lam_re, lam_im, log_dt, b_re, b_im, c_re, c_im, t // S5_CHUNK)
    x2d = x.reshape(n, d)
    u = _norm_call(x2d, gpre, F32, tm)
    y = _s5_scan_call(u.reshape(b, t, d), ka, ps, qt, a1, a2, n_steps)
    out = _s5_out_call(x2d, y.reshape(n, d), gpre, d_skip, w_glu.astype(BF16), gpost, tm)
    return out.reshape(b, t, d)


def _ffn_kernel(xprev_ref, x_ref, gpre_ref, wg_ref, wv_ref, cw_ref, cb_ref, wo_ref, gpost_ref,
                o_ref, xn_ref, acc_ref, *, seq_tiles):
    i = pl.program_id(0)
    j = pl.program_id(1)
    h = CONV_HALO

    @pl.when(j == 0)
    def _():
        g = gpre_ref[...]
        xn_ref[h:, :] = _rms(x_ref[...], g).astype(BF16)
        keep = jnp.where(i % seq_tiles == 0, 0.0, 1.0)
        xn_ref[:h, :] = (_rms(xprev_ref[...], g) * keep).astype(BF16)
        acc_ref[...] = jnp.zeros_like(acc_ref)

    xn = xn_ref[...]
    gate = _dot(xn, wg_ref[...])
    val = _dot(xn[h:], wv_ref[...])
    cw = cw_ref[...]
    conv = (cw[0:1] * pltpu.roll(gate, 2, 0) + cw[1:2] * pltpu.roll(gate, 1, 0)
            + cw[2:3] * gate + cb_ref[...])
    act = _gelu(conv[h:]) * val
    acc_ref[...] += _dot(act.astype(BF16), wo_ref[...])

    @pl.when(j == pl.num_programs(1) - 1)
    def _():
        o_ref[...] = x_ref[...] + _rms(acc_ref[...], gpost_ref[...])


def _ffn_layer(x, gpre, gpost, w_in, conv_w, conv_b, w_out):
    b, t, d = x.shape
    f = w_out.shape[0]
    n = b * t
    tm = _row_tile(t, FFN_ROWS)
    tf = max(w for w in range(LANES, min(f, FFN_HIDDEN_CAP) + 1, LANES) if f % w == 0)
    assert f % tf == 0 and tm % CONV_HALO == 0
    nf = f // tf
    h = CONV_HALO
    x2d = x.reshape(n, d)
    cw = jnp.zeros((8, f), F32).at[:CONV_WIDTH].set(conv_w.astype(F32))
    w_in16 = w_in.astype(BF16)
    out = pl.pallas_call(
        functools.partial(_ffn_kernel, seq_tiles=t // tm),
        grid=(n // tm, nf),
        in_specs=[pl.BlockSpec((h, d), lambda i, j: (jnp.maximum(i * (tm // h) - 1, 0), 0)),
                  pl.BlockSpec((tm, d), lambda i, j: (i, 0)),
                  pl.BlockSpec((1, d), lambda i, j: (0, 0)),
                  pl.BlockSpec((d, tf), lambda i, j: (0, j)),
                  pl.BlockSpec((d, tf), lambda i, j: (0, nf + j)),
                  pl.BlockSpec((8, tf), lambda i, j: (0, j)),
                  pl.BlockSpec((1, tf), lambda i, j: (0, j)),
                  pl.BlockSpec((tf, d), lambda i, j: (j, 0)),
                  pl.BlockSpec((1, d), lambda i, j: (0, 0))],
        out_specs=pl.BlockSpec((tm, d), lambda i, j: (i, 0)),
        out_shape=jax.ShapeDtypeStruct((n, d), F32),
        scratch_shapes=[pltpu.VMEM((tm + h, d), BF16), pltpu.VMEM((tm, d), F32)],
        compiler_params=_cparams(2),
        name="conv_ffn",
    )(x2d, x2d, gpre.reshape(1, d), w_in16, w_in16, cw, conv_b.reshape(1, f).astype(F32),
      w_out.astype(BF16), gpost.reshape(1, d))
    return out.reshape(b, t, d)


def _rope_tables(pos, width):
    half = ROPE_DIM // 2
    inv = ROPE_THETA ** (-jnp.arange(half, dtype=F32) / half)
    ang = pos.astype(F32)[:, None] * inv[None, :]
    cos, sin = jnp.cos(ang), jnp.sin(ang)
    rest = HEAD_DIM - ROPE_DIM
    n = pos.shape[0]
    c = jnp.concatenate([cos, cos, jnp.ones((n, rest), F32)], axis=-1)
    s = jnp.concatenate([-sin, sin, jnp.zeros((n, rest), F32)], axis=-1)
    reps = width // HEAD_DIM
    return jnp.tile(c, (1, reps)), jnp.tile(s, (1, reps))


def _swap_perm(width):
    half = ROPE_DIM // 2
    d = np.arange(width)
    dd = d % HEAD_DIM
    return np.where(dd < half, d + half, np.where(dd < ROPE_DIM, d - half, d))


def _kv_kernel(x_ref, g_ref, wkt_ref, w_ref, cost_ref, sint_ref, kt_ref, v_ref, vs_ref, c_ref):
    wk = v_ref.shape[-1]
    sn = _rms(x_ref[...], g_ref[...]).astype(BF16)
    rt = _dot_nt(wkt_ref[...], sn)
    reps = wk // LANES
    cos = jnp.concatenate([cost_ref[...]] * reps, axis=0)
    sin = jnp.concatenate([sint_ref[...]] * reps, axis=0)
    kt_ref[0] = (rt[:wk] * cos + rt[wk:] * sin).astype(kt_ref.dtype)
    r = _dot(sn, w_ref[...])
    v_ref[...] = r[:, :wk].astype(v_ref.dtype)
    vs_ref[...] = r[:, wk:2 * wk].astype(vs_ref.dtype)
    c_ref[...] = r[:, 2 * wk:]


def _kv_call(x2d, g, wkt, w_rest, cos_t, sin_t, b, t, tm):
    n, d = x2d.shape
    wk = KV_HEADS * 2 * HEAD_DIM
    tpt = t // tm
    row = lambda i: (i, 0)
    fix = lambda i: (0, 0)
    tab = lambda i: (0, i % tpt)
    return pl.pallas_call(
        _kv_kernel,
        grid=(n // tm,),
        in_specs=[pl.BlockSpec((tm, d), row), pl.BlockSpec((1, d), fix),
                  pl.BlockSpec(wkt.shape, fix), pl.BlockSpec(w_rest.shape, fix),
                  pl.BlockSpec((LANES, tm), tab), pl.BlockSpec((LANES, tm), tab)],
        out_specs=[pl.BlockSpec((1, wk, tm), lambda i: (i // tpt, 0, i % tpt)),
                   pl.BlockSpec((tm, wk), row), pl.BlockSpec((tm, wk), row), pl.BlockSpec((tm, wk), row)],
        out_shape=[jax.ShapeDtypeStruct((b, wk, t), BF16), jax.ShapeDtypeStruct((n, wk), BF16),
                   jax.ShapeDtypeStruct((n, wk), BF16), jax.ShapeDtypeStruct((n, wk), F32)],
        compiler_params=_cparams(1),
        name="nsa_kv_proj",
    )(x2d, g.reshape(1, d), wkt, w_rest, cos_t, sin_t)


def _cmp_kernel(z_ref, pe_ref, w1_ref, w2kt_ref, w2v_ref, cost_ref, sint_ref, kt_ref, v_ref, *, n_cmp):
    st = CMP_STRIDE
    c = z_ref.shape[1] // st
    hid = w1_ref.shape[-1] // 2
    z = jnp.concatenate([z_ref[0, pl.ds(l, c, stride=st), :] for l in range(st)], axis=1)
    top = _dot((z + pe_ref[0:1, :]).astype(BF16), w1_ref[0])
    bot = _dot((z + pe_ref[1:2, :]).astype(BF16), w1_ref[1])
    hdn = _gelu(top + pltpu.roll(bot, c - 1, 0)).astype(BF16)
    res = [hdn[:, :hid], hdn[:, hid:]]
    kt = (_dot_nt(w2kt_ref[0], res[0]) * cost_ref[...] + _dot_nt(w2kt_ref[1], res[0]) * sint_ref[...])
    col = lax.broadcasted_iota(jnp.int32, kt.shape, 1)
    kt_ref[0, 0] = jnp.where(col < n_cmp, kt, 0.0).astype(kt_ref.dtype)
    v = _dot(res[1], w2v_ref[...])
    row = lax.broadcasted_iota(jnp.int32, v.shape, 0)
    v_ref[0, 0] = jnp.where(row < n_cmp, v, 0.0).astype(v_ref.dtype)


def _cmp_call(ctok, pe, w1, w2kt, w2v, cos_t, sin_t, n_cmp):
    b, t, wid = ctok.shape
    g = wid // LANES
    c = t // CMP_STRIDE
    fix2 = lambda i, j: (0, 0)
    fix3 = lambda i, j: (0, 0, 0)
    return pl.pallas_call(
        functools.partial(_cmp_kernel, n_cmp=n_cmp),
        grid=(b, g),
        in_specs=[pl.BlockSpec((1, t, LANES), lambda i, j: (i, 0, j)),
                  pl.BlockSpec(pe.shape, fix2), pl.BlockSpec(w1.shape, fix3),
                  pl.BlockSpec(w2kt.shape, fix3), pl.BlockSpec(w2v.shape, fix2),
                  pl.BlockSpec(cos_t.shape, fix2), pl.BlockSpec(sin_t.shape, fix2)],
        out_specs=[pl.BlockSpec((1, 1, LANES, c), lambda i, j: (i, j, 0, 0)),
                   pl.BlockSpec((1, 1, c, LANES), lambda i, j: (i, j, 0, 0))],
        out_shape=[jax.ShapeDtypeStruct((b, g, LANES, c), BF16), jax.ShapeDtypeStruct((b, g, c, LANES), BF16)],
        compiler_params=_cparams(2),
        name="nsa_compress",
    )(ctok, pe, w1, w2kt, w2v, cos_t, sin_t)


def _q_kernel(x_ref, g_ref, w_ref, cos_ref, sin_ref, q_ref, gate_ref):
    d = q_ref.shape[-1]
    hn = _rms(x_ref[...], g_ref[...]).astype(BF16)
    r = _dot(hn, w_ref[...])
    reps = d // LANES
    cos = jnp.concatenate([cos_ref[...]] * reps, axis=1)
    sin = jnp.concatenate([sin_ref[...]] * reps, axis=1)
    q_ref[...] = ((r[:, :d] * cos + r[:, d:2 * d] * sin) * (HEAD_DIM ** -0.5 * LOG2E)).astype(q_ref.dtype)
    gate_ref[...] = jax.nn.sigmoid(r[:, 2 * d:])


def _q_call(x2d, g, w_all, cos, sin, t, tm):
    n, d = x2d.shape
    dq = N_HEADS * HEAD_DIM
    tpt = t // tm
    row = lambda i: (i, 0)
    fix = lambda i: (0, 0)
    tab = lambda i: (i % tpt, 0)
    return pl.pallas_call(
        _q_kernel,
        grid=(n // tm,),
        in_specs=[pl.BlockSpec((tm, d), row), pl.BlockSpec((1, d), fix),
                  pl.BlockSpec(w_all.shape, fix),
                  pl.BlockSpec((tm, LANES), tab), pl.BlockSpec((tm, LANES), tab)],
        out_specs=[pl.BlockSpec((tm, dq), row), pl.BlockSpec((tm, LANES), row)],
        out_shape=[jax.ShapeDtypeStruct((n, dq), BF16), jax.ShapeDtypeStruct((n, LANES), F32)],
        compiler_params=_cparams(1),
        name="nsa_q_proj",
    )(x2d, g.reshape(1, d), w_all, cos, sin)


ATTN_UNROLL = 4
ATTN_Q_ROWS = 256


def _attn_kernel(q_ref, gate_ref, kct_ref, vc_ref, kt_ref, v2_ref, v2s_ref, ovt_ref, esel_ref, eg_ref,
                 o_ref, kcbd_ref, vcbd_ref, kbd_ref, vbd_ref, sc_ref, *, n_sb, top_n):
    qb = q_ref.shape[1]
    rep = N_HEADS // KV_HEADS
    hd = HEAD_DIM
    n_c = kct_ref.shape[-1]
    n_tiles = kt_ref.shape[-1] // LANES
    i = pl.program_id(2)
    start = i * qb

    @pl.when(i == 0)
    def _():
        kcbd_ref[...] = jnp.zeros_like(kcbd_ref)
        vcbd_ref[...] = jnp.zeros_like(vcbd_ref)
        kbd_ref[...] = jnp.zeros_like(kbd_ref)
        vbd_ref[...] = jnp.zeros_like(vbd_ref)
        vc = vc_ref[0, 0]
        for r in range(rep):
            h = hd * (r % 2)
            kcbd_ref[hd * r:hd * (r + 1), n_c * r:n_c * (r + 1)] = kct_ref[0, 0, 0:hd, :]
            vcbd_ref[n_c * r:n_c * (r + 1), hd * r:hd * (r + 1)] = vc[:, h:h + hd]

    qblk = q_ref[0]

    s1 = _dot(qblk, kcbd_ref[...])
    n_idx = lax.broadcasted_iota(jnp.int32, (qb, n_c), 1)
    t1 = start + lax.broadcasted_iota(jnp.int32, (qb, n_c), 0)
    m1 = n_idx * CMP_STRIDE + (CMP_BLOCK - 1) <= t1
    psum = jnp.zeros((qb, n_c), F32)
    p_parts = []
    for r in range(rep):
        sm = jnp.where(m1, s1[:, n_c * r:n_c * (r + 1)], NEG)
        mrow = jnp.max(sm, axis=1, keepdims=True)
        e = jnp.exp2(sm - jnp.where(mrow > 0.5 * NEG, mrow, 0.0))
        l = jnp.sum(e, axis=1, keepdims=True)
        p = e * (1.0 / jnp.where(l > 0.0, l, 1.0))
        psum = psum + p
        p_parts.append(p.astype(BF16))
    o_c = _dot(jnp.concatenate(p_parts, axis=1), vcbd_ref[...])

    p_hi = psum.astype(BF16)
    p_lo = (psum - p_hi.astype(F32)).astype(BF16)
    imp_t = _dot_nt(ovt_ref[...], p_hi) + _dot_nt(ovt_ref[...], p_lo)
    blk = lax.broadcasted_iota(jnp.int32, (n_sb, qb), 0)
    t_l = start + lax.broadcasted_iota(jnp.int32, (n_sb, qb), 1)
    cur = jnp.right_shift(t_l, int(math.log2(SEL_BLOCK)))
    forced = (blk == 0) | (blk == cur) | (blk == cur - 1)
    score = jnp.where(forced, SEL_FORCE, jnp.where(blk <= cur, imp_t[:n_sb], SEL_NEG))
    blk_f = blk.astype(F32)
    work = score
    sel_t = jnp.zeros((n_sb, qb), F32)
    for _ in range(top_n):
        best = jnp.max(work, axis=0, keepdims=True)
        first = jnp.min(jnp.where(work == best, blk_f, float(n_sb)), axis=0, keepdims=True)
        hit = blk_f == first
        sel_t = jnp.where(hit, 1.0, sel_t)
        work = jnp.where(hit, -jnp.inf, work)
    sel_t = jnp.where(score > 0.5 * SEL_NEG, sel_t, 0.0)
    sel = sel_t.T.astype(BF16)

    lane = lax.broadcasted_iota(jnp.int32, (qb, LANES), 1)
    t_q = start + lax.broadcasted_iota(jnp.int32, (qb, LANES), 0)

    def scores(u, off, row0):
        ktile = kt_ref[0, row0:row0 + hd, pl.ds(off, LANES)]
        for r in range(rep):
            kbd_ref[u, hd * r:hd * (r + 1), LANES * r:LANES * (r + 1)] = ktile
        return _dot(qblk, kbd_ref[u])

    def weighted_values(u, off, ps, v_even, v_odd):
        ve = v_even[0, pl.ds(off, LANES), :]
        vo = v_odd[0, pl.ds(off, LANES), :]
        for r in range(rep):
            h = hd * (r % 2)
            src = ve if r % 2 == 0 else vo
            vbd_ref[u, LANES * r:LANES * (r + 1), hd * r:hd * (r + 1)] = src[:, h:h + hd]
        return _dot(jnp.concatenate(ps, axis=1), vbd_ref[u])

    def seg(a, r):
        return a[:, LANES * r:LANES * (r + 1)]

    def row_bcast(parts, op):
        return [jnp.broadcast_to(op(a, axis=1, keepdims=True), (qb, LANES)) for a in parts]

    def normalise(acc, ls):
        lrow = row_bcast(ls, jnp.sum)
        den = jnp.concatenate([jnp.where(lane < hd, lrow[2 * c], lrow[2 * c + 1]) for c in range(rep // 2)],
                              axis=1)
        return acc / den

    tpq = qb // LANES
    n_w = WINDOW // LANES + tpq
    w_first = i * tpq - WINDOW // LANES

    def win_mask(w):
        kpos = (w_first + w) * LANES + lane
        if w >= n_w - tpq:
            return kpos <= t_q
        if w >= tpq:
            return kpos >= 0
        return (kpos >= 0) & (t_q - kpos < WINDOW)

    w_off = [pl.multiple_of(jnp.maximum(w_first + w, 0) * LANES, LANES) for w in range(n_w)]
    mx = [jnp.full((qb, LANES), NEG, F32) for _ in range(rep)]
    for w in range(n_w):
        s = scores(w, w_off[w], hd)
        mk = win_mask(w)
        sm = [jnp.where(mk, seg(s, r), NEG) for r in range(rep)]
        sc_ref[w] = jnp.concatenate(sm, axis=1)
        mx = [jnp.maximum(mx[r], sm[r]) for r in range(rep)]
    mrow = row_bcast(mx, jnp.max)
    ls = [jnp.zeros((qb, LANES), F32) for _ in range(rep)]
    acc = jnp.zeros((qb, rep * hd), F32)
    for w in range(n_w):
        ps = [jnp.exp2(sc_ref[w, :, LANES * r:LANES * (r + 1)] - mrow[r]) for r in range(rep)]
        ls = [ls[r] + ps[r] for r in range(rep)]
        acc = acc + weighted_values(w, w_off[w], [p.astype(BF16) for p in ps], v2s_ref, v2_ref)
    o_w = normalise(acc, ls)

    gw = rep * hd
    g_hi = gate_ref[0].astype(BF16)
    g_lo = (gate_ref[0] - g_hi.astype(F32)).astype(BF16)
    gx = _dot(g_hi, eg_ref[0]) + _dot(g_lo, eg_ref[0])
    o_cw = gx[:, :gw] * o_c + gx[:, 2 * gw:] * o_w
    g_s = gx[:, gw:2 * gw]

    n_it = lax.div((i + 1) * tpq + ATTN_UNROLL - 1, ATTN_UNROLL)

    def tile(p, u):
        kt = p * ATTN_UNROLL + u
        return kt, pl.multiple_of(jnp.minimum(kt, n_tiles - 1) * LANES, LANES)

    def pass1(p, mx):
        mx = list(mx)
        for u in range(ATTN_UNROLL):
            kt, off = tile(p, u)
            picked = _dot(sel, esel_ref[0:n_sb, pl.ds(off, LANES)])
            mk = (picked > 0.5) & (kt * LANES + lane <= t_q)
            s = scores(u, off, 0)
            sm = [jnp.where(mk, seg(s, r), NEG) for r in range(rep)]
            sc_ref[kt] = jnp.concatenate(sm, axis=1)
            mx = [jnp.maximum(mx[r], sm[r]) for r in range(rep)]
        return tuple(mx)

    mx = lax.fori_loop(0, n_it, pass1, tuple(jnp.full((qb, LANES), NEG, F32) for _ in range(rep)))
    mrow = row_bcast(mx, jnp.max)

    def pass2(p, carry):
        ls, acc = carry
        ls = list(ls)
        for u in range(ATTN_UNROLL):
            kt, off = tile(p, u)
            ps = []
            for r in range(rep):
                e = jnp.exp2(sc_ref[kt, :, LANES * r:LANES * (r + 1)] - mrow[r])
                ls[r] = ls[r] + e
                ps.append(e.astype(BF16))
            acc = acc + weighted_values(u, off, ps, v2_ref, v2s_ref)
        return tuple(ls), acc

    ls, acc = lax.fori_loop(0, n_it, pass2, (tuple(jnp.zeros((qb, LANES), F32) for _ in range(rep)),
                                             jnp.zeros((qb, rep * hd), F32)))
    o_ref[0] = (o_cw + g_s * normalise(acc, ls)).astype(o_ref.dtype)


def _attn_call(q, gates, kct, vc, kt, v2, v2s, overlap_t, esel, expand, n_sb):
    b, t, dq = q.shape
    g = KV_HEADS
    gw = dq // g
    rep = N_HEADS // KV_HEADS
    c = kct.shape[-1]
    qb = min(ATTN_Q_ROWS, t)
    assert qb % LANES == 0 and t % qb == 0 and rep % 2 == 0 and gw == 2 * LANES
    blk_q = pl.BlockSpec((1, qb, gw), lambda bi, gi, i: (bi, i, gi))
    blk_v = pl.BlockSpec((1, t, LANES), lambda bi, gi, i: (bi, 0, gi))
    fix2 = lambda bi, gi, i: (0, 0)
    n_slots = t // LANES + ATTN_UNROLL
    n_bd = max(ATTN_UNROLL, WINDOW // LANES + qb // LANES)
    return pl.pallas_call(
        functools.partial(_attn_kernel, n_sb=n_sb, top_n=min(SEL_TOPN, n_sb)),
        grid=(b, g, t // qb),
        in_specs=[blk_q,
                  pl.BlockSpec((1, qb, LANES), lambda bi, gi, i: (bi, i, 0)),
                  pl.BlockSpec((1, 1, LANES, c), lambda bi, gi, i: (bi, gi, 0, 0)),
                  pl.BlockSpec((1, 1, c, LANES), lambda bi, gi, i: (bi, gi, 0, 0)),
                  pl.BlockSpec((1, LANES, t), lambda bi, gi, i: (bi, gi, 0)),
                  blk_v, blk_v,
                  pl.BlockSpec(overlap_t.shape, fix2), pl.BlockSpec(esel.shape, fix2),
                  pl.BlockSpec((1,) + expand.shape[1:], lambda bi, gi, i: (gi, 0, 0))],
        out_specs=blk_q,
        out_shape=jax.ShapeDtypeStruct((b, t, dq), BF16),
        scratch_shapes=[pltpu.VMEM((rep * HEAD_DIM, rep * c), BF16),
                        pltpu.VMEM((rep * c, rep * HEAD_DIM), BF16),
                        pltpu.VMEM((n_bd, rep * HEAD_DIM, rep * LANES), BF16),
                        pltpu.VMEM((n_bd, rep * LANES, rep * HEAD_DIM), BF16),
                        pltpu.VMEM((n_slots, qb, rep * LANES), F32)],
        compiler_params=_cparams(3),
        name="nsa_attention",
    )(q, gates, kct, vc, kt, v2, v2s, overlap_t, esel, expand)


def _oproj_kernel(x_ref, o_ref, w_ref, gpost_ref, out_ref):
    out_ref[...] = x_ref[...] + _rms(_dot(o_ref[...], w_ref[...]), gpost_ref[...])


def _oproj_call(x2d, o, w_o, gpost, tm):
    n, d = x2d.shape
    dq = o.shape[-1]
    row = lambda i: (i, 0)
    fix = lambda i: (0, 0)
    return pl.pallas_call(
        _oproj_kernel,
        grid=(n // tm,),
        in_specs=[pl.BlockSpec((tm, d), row), pl.BlockSpec((tm, dq), row),
                  pl.BlockSpec(w_o.shape, fix), pl.BlockSpec((1, d), fix)],
        out_specs=pl.BlockSpec((tm, d), row),
        out_shape=jax.ShapeDtypeStruct((n, d), F32),
        compiler_params=_cparams(1),
        name="nsa_out_proj",
    )(x2d, o, w_o, gpost.reshape(1, d))


def _shared_kv(x, kv_norm_g, w_kv, pe_k, pe_v, k_w1, k_w2, v_w1, v_w2):
    b, t, d = x.shape
    n = b * t
    g, hd = KV_HEADS, HEAD_DIM
    tm = _row_tile(t, 512)
    def cols(ta, tb):
        return np.concatenate([np.concatenate([np.arange(hd) + ta * g * hd + gi * hd,
                                               np.arange(hd) + tb * g * hd + gi * hd]) for gi in range(g)])
    k_cols = cols(2, 4)
    wkt = jnp.concatenate([w_kv[:, k_cols], w_kv[:, k_cols[_swap_perm(k_cols.size)]]], axis=1).T.astype(BF16)
    w_rest = jnp.concatenate([w_kv[:, cols(3, 5)], w_kv[:, cols(5, 3)], w_kv[:, cols(0, 1)]], axis=1).astype(BF16)
    cos, sin = _rope_tables(jnp.arange(t), LANES)
    kt, v2, v2s, ctok = _kv_call(x.reshape(n, d), kv_norm_g, wkt, w_rest, cos.T, sin.T, b, t, tm)
    v2 = v2.reshape(b, t, g * 2 * hd)
    v2s = v2s.reshape(b, t, g * 2 * hd)
    c = t // CMP_STRIDE
    n_cmp = c - CMP_BLOCK // CMP_STRIDE + 1
    assert CMP_BLOCK == 2 * CMP_STRIDE
    st, hid = CMP_STRIDE, k_w1.shape[-1]
    pe = jnp.stack([pe_k.reshape(2, st, hd), pe_v.reshape(2, st, hd)], axis=2).reshape(2, st * 2 * hd).astype(F32)
    zw = jnp.zeros((2, st, hd, hid), k_w1.dtype)
    w1 = jnp.stack([jnp.concatenate([k_w1.reshape(2, st, hd, hid), zw], axis=-1),
                    jnp.concatenate([zw, v_w1.reshape(2, st, hd, hid)], axis=-1)], axis=2)
    w1 = w1.reshape(2, st * 2 * hd, 2 * hid).astype(BF16)
    zpad = jnp.zeros_like(k_w2)
    w2kt = jnp.stack([jnp.concatenate([k_w2, zpad], axis=1).T,
                      jnp.concatenate([k_w2[:, _swap_perm(hd)], zpad], axis=1).T]).astype(BF16)
    w2v = jnp.concatenate([v_w2, v_w2], axis=1).astype(BF16)
    pos_c = jnp.arange(c) * CMP_STRIDE + CMP_BLOCK - 1
    cos_c, sin_c = _rope_tables(pos_c, hd)
    zlane = jnp.zeros((c, LANES - hd), F32)
    cos_c = jnp.concatenate([cos_c, zlane], axis=1).T
    sin_c = jnp.concatenate([sin_c, zlane], axis=1).T
    kct, vc = _cmp_call(ctok.reshape(b, t, g * 2 * hd), pe, w1, w2kt, w2v, cos_c, sin_c, n_cmp)
    return kct, vc, kt, v2, v2s, n_cmp


def _nsa_layer(x, gpre, gpost, w_q, w_o, kct, vc, kt, v2, v2s, n_cmp):
    b, t, d = x.shape
    n = b * t
    hd = HEAD_DIM
    dq = N_HEADS * hd
    tm = _row_tile(t, 512)
    wg = jnp.zeros((d, LANES), w_q.dtype).at[:, :3 * N_HEADS].set(w_q[:, dq:])
    w_all = jnp.concatenate([w_q[:, :dq], w_q[:, :dq][:, _swap_perm(dq)], wg], axis=1).astype(BF16)
    cos, sin = _rope_tables(jnp.arange(t), LANES)
    q, gates = _q_call(x.reshape(n, d), gpre, w_all, cos, sin, t, tm)
    n_sb = t // SEL_BLOCK
    assert n_sb % 8 == 0 and n_sb <= LANES
    c = vc.shape[2]
    ci = np.arange(c)[None, :]
    sj = np.arange(LANES)[:, None]
    overlap_t = ((ci * CMP_STRIDE < (sj + 1) * SEL_BLOCK) & (ci * CMP_STRIDE + CMP_BLOCK > sj * SEL_BLOCK)
                 & (ci < n_cmp) & (sj < n_sb)).astype(np.float32)
    esel = (np.arange(LANES)[:, None] == (np.arange(t)[None, :] // SEL_BLOCK)).astype(np.float32)
    gw = dq // KV_HEADS
    expand = np.zeros((KV_HEADS, LANES, 3 * gw), np.float32)
    for gi in range(KV_HEADS):
        hh = gi * (N_HEADS // KV_HEADS) + np.arange(gw) // hd
        for j in range(3):
            expand[gi, hh * 3 + j, j * gw + np.arange(gw)] = 1.0
    o = _attn_call(q.reshape(b, t, dq), gates.reshape(b, t, LANES), kct, vc, kt, v2, v2s,
                   jnp.asarray(overlap_t, dtype=BF16), jnp.asarray(esel, dtype=BF16),
                   jnp.asarray(expand, dtype=BF16), n_sb)
    out = _oproj_call(x.reshape(n, d), o.reshape(n, dq), w_o.astype(BF16), gpost, tm)
    return out.reshape(b, t, d)


def kernel(x, a_lam_re, a_lam_im, a_log_dt, a_b_re, a_b_im, a_c_re, a_c_im, a_d, a_w_glu, b_w_q, b_w_o, kv_norm_g, w_kv, cmp_pe_k, cmp_pe_v, cmp_k_w1, cmp_k_w2, cmp_v_w1, cmp_v_w2, mix_pre_g, mix_post_g, ffn_pre_g, ffn_post_g, ffn_w_in, ffn_conv_w, ffn_conv_b, ffn_w_out):
    depth = mix_pre_g.shape[0]
    n_a = depth // 2
    kv = None
    for layer in range(depth):
        if layer < n_a:
            i = layer
            x = _s5_layer(x, mix_pre_g[layer], mix_post_g[layer], a_lam_re[i], a_lam_im[i], a_log_dt[i],
                          a_b_re[i], a_b_im[i], a_c_re[i], a_c_im[i], a_d[i], a_w_glu[i])
        else:
            j = layer - n_a
            x = _nsa_layer(x, mix_pre_g[layer], mix_post_g[layer], b_w_q[j], b_w_o[j], *kv)
        x = _ffn_layer(x, ffn_pre_g[layer], ffn_post_g[layer], ffn_w_in[layer], ffn_conv_w[layer],
                       ffn_conv_b[layer], ffn_w_out[layer])
        if layer == n_a - 1:
            kv = _shared_kv(x, kv_norm_g, w_kv, cmp_pe_k, cmp_pe_v, cmp_k_w1, cmp_k_w2, cmp_v_w1, cmp_v_w2)
    return x
```

```python
import functools
import math

import numpy as np
import jax
import jax.numpy as jnp
from jax import lax
from jax.experimental import pallas as pl
from jax.experimental.pallas import tpu as pltpu

F32 = jnp.float32
BF16 = jnp.bfloat16

S5_GROUP = 16
S5_STATE = 64
N_HEADS = 16
KV_HEADS = 4
HEAD_DIM = 64
CMP_BLOCK = 32
CMP_STRIDE = 16
SEL_BLOCK = 64
SEL_TOPN = 16
WINDOW = 512
ROPE_THETA = 500000.0
ROPE_DIM = HEAD_DIM // 4
CONV_WIDTH = 3
EPS = 1e-6
NEG = -1e30
SEL_FORCE = 1e4
SEL_NEG = -1e4
LOG2E = math.log2(math.e)

LANES = 128
MXU_TILE = 256
S5_CHUNK = 16
CONV_HALO = 16
FFN_ROWS = 1024
FFN_HIDDEN_CAP = 1408
VMEM_LIMIT = 48 * 1024 * 1024


def _cparams(n_axes):
    return pltpu.CompilerParams(dimension_semantics=("arbitrary",) * n_axes,
                                vmem_limit_bytes=VMEM_LIMIT)


def _rms(x, g):
    return x * lax.rsqrt(jnp.mean(x * x, axis=-1, keepdims=True) + EPS) * g


def _gelu(x):
    return jax.nn.gelu(x, approximate=True)


def _dot(a, b):
    return jnp.dot(a, b, preferred_element_type=F32)


def _dot_nt(a, b):
    return lax.dot_general(a, b, (((1,), (1,)), ((), ())), preferred_element_type=F32)


def _row_tile(n, want):
    t = min(n, want)
    assert n % t == 0
    return t


def _s5_tables(lam_re, lam_im, log_dt, b_re, b_im, c_re, c_im, n_chunks):
    L = S5_CHUNK
    G, P = lam_re.shape
    I = b_re.shape[-1]
    dt = jnp.exp(log_dt.astype(F32))[:, None]
    lr, li = lam_re.astype(F32), lam_im.astype(F32)
    mag = jnp.exp(lr * dt)
    ab_re, ab_im = mag * jnp.cos(li * dt), mag * jnp.sin(li * dt)
    nr, ni = ab_re - 1.0, ab_im
    den = lr * lr + li * li
    coef_re = (nr * lr + ni * li) / den
    coef_im = (ni * lr - nr * li) / den
    br, bi = b_re.astype(F32), b_im.astype(F32)
    bb_re = coef_re[..., None] * br - coef_im[..., None] * bi
    bb_im = coef_re[..., None] * bi + coef_im[..., None] * br
    pr = [jnp.ones_like(ab_re)]
    pi = [jnp.zeros_like(ab_re)]
    for _ in range(L):
        r, i = pr[-1], pi[-1]
        pr.append(r * ab_re - i * ab_im)
        pi.append(r * ab_im + i * ab_re)
    pw_re = jnp.stack(pr)
    pw_im = jnp.stack(pi)
    cr, ci = c_re.astype(F32), c_im.astype(F32)
    cl_re = cr[None] * pw_re[:, :, None, :] - ci[None] * pw_im[:, :, None, :]
    cl_im = cr[None] * pw_im[:, :, None, :] + ci[None] * pw_re[:, :, None, :]
    kk = jnp.einsum('kgop,gpi->gkio', jnp.concatenate([cl_re[:L], -cl_im[:L]], axis=-1),
                    jnp.concatenate([bb_re, bb_im], axis=1))
    rev_re = pw_re[L - 1 - np.arange(L)]
    rev_im = pw_im[L - 1 - np.arange(L)]
    pb_re = rev_re[..., None] * bb_re[None] - rev_im[..., None] * bb_im[None]
    pb_im = rev_re[..., None] * bb_im[None] + rev_im[..., None] * bb_re[None]
    gl = LANES // I
    nt = G // gl
    ka = kk.reshape(nt, gl, L, I, I).transpose(0, 2, 1, 3, 4).reshape(nt, L * LANES, I).astype(BF16)
    ps = jnp.concatenate([pb_re, pb_im], axis=2).reshape(L, nt, gl, 2 * P, I)
    ps = ps.transpose(1, 0, 2, 4, 3).reshape(nt, L * LANES, 2 * P).astype(BF16)
    qt = jnp.concatenate([cl_re[1:], -cl_im[1:]], axis=-1).reshape(L, nt, gl, I, 2 * P)
    qt = qt.transpose(1, 0, 2, 3, 4).reshape(nt, L * LANES, 2 * P).astype(BF16)
    n_steps = int(math.ceil(math.log2(n_chunks))) if n_chunks > 1 else 0
    mr, mi = pw_re[L], pw_im[L]
    a1, a2 = [], []
    for _ in range(max(n_steps, 1)):
        a1.append(jnp.concatenate([mr, mr], axis=-1).reshape(nt, gl * 2 * P))
        a2.append(jnp.concatenate([-mi, mi], axis=-1).reshape(nt, gl * 2 * P))
        mr, mi = mr * mr - mi * mi, 2.0 * mr * mi
    pad = (-len(a1)) % 8
    a1 = jnp.stack(a1 + [jnp.zeros_like(a1[0])] * pad, axis=1)
    a2 = jnp.stack(a2 + [jnp.zeros_like(a2[0])] * pad, axis=1)
    return ka, ps, qt, a1, a2, n_steps


def _norm_kernel(x_ref, g_ref, o_ref):
    o_ref[...] = _rms(x_ref[...], g_ref[...]).astype(o_ref.dtype)


def _norm_call(x2d, g, out_dtype, tm):
    n, d = x2d.shape
    return pl.pallas_call(
        _norm_kernel,
        grid=(n // tm,),
        in_specs=[pl.BlockSpec((tm, d), lambda i: (i, 0)),
                  pl.BlockSpec((1, d), lambda i: (0, 0))],
        out_specs=pl.BlockSpec((tm, d), lambda i: (i, 0)),
        out_shape=jax.ShapeDtypeStruct((n, d), out_dtype),
        compiler_params=_cparams(1),
        name="s5_prenorm",
    )(x2d, g.reshape(1, d))


def _s5_scan_kernel(u_ref, ka_ref, ps_ref, qt_ref, rep_ref, own_ref, a1_ref, a2_ref, y_ref,
                    wy_ref, ws_ref, wq_ref, *, n_steps):
    L = S5_CHUNK
    I = S5_GROUP
    gl = LANES // I
    n_chunks = u_ref.shape[1] // L

    @pl.when(pl.program_id(1) == 0)
    def _():
        lag = _dot(ka_ref[0], rep_ref[...])
        own = own_ref[...]
        lag = [(lag[LANES * k:LANES * (k + 1)] * own).astype(BF16) for k in range(L)]
        wy_ref[...] = jnp.zeros_like(wy_ref)
        ws_ref[...] = jnp.zeros_like(ws_ref)
        wq_ref[...] = jnp.zeros_like(wq_ref)
        for s in range(L):
            for r in range(s, L):
                wy_ref[LANES * s:LANES * (s + 1), LANES * r:LANES * (r + 1)] = lag[r - s]
            for g in range(gl):
                rows = slice(LANES * s + I * g, LANES * s + I * (g + 1))
                ws_ref[rows, LANES * g:LANES * (g + 1)] = ps_ref[0, rows, :]
                wq_ref[rows, LANES * g:LANES * (g + 1)] = qt_ref[0, rows, :]

    ucat = jnp.concatenate([u_ref[0, pl.ds(s, n_chunks, stride=L), :] for s in range(L)], axis=1).astype(BF16)
    y_intra = jnp.concatenate([_dot(ucat[:, :c1], wy_ref[:c1, c1 - MXU_TILE:c1])
                               for c1 in range(MXU_TILE, L * LANES + 1, MXU_TILE)], axis=1)
    ends = _dot(ucat, ws_ref[...])
    row = lax.broadcasted_iota(jnp.int32, (n_chunks, LANES), 0)
    xprev = []
    for g in range(gl):
        x = ends[:, LANES * g:LANES * (g + 1)]
        for j in range(n_steps):
            k = 1 << j
            sh = jnp.where(row >= k, pltpu.roll(x, k, 0), 0.0)
            x = (x + a1_ref[0, j:j + 1, LANES * g:LANES * (g + 1)] * sh
                 + a2_ref[0, j:j + 1, LANES * g:LANES * (g + 1)] * pltpu.roll(sh, S5_STATE, 1))
        xprev.append(jnp.where(row >= 1, pltpu.roll(x, 1, 0), 0.0).astype(BF16))
    y = y_intra + _dot_nt(jnp.concatenate(xprev, axis=1), wq_ref[...])
    for s in range(L):
        y_ref[0, pl.ds(s, n_chunks, stride=L), :] = y[:, LANES * s:LANES * (s + 1)]


def _s5_scan_call(u, ka, ps, qt, a1, a2, n_steps):
    b, t, d = u.shape
    nt = d // LANES
    gl = LANES // S5_GROUP
    wide = S5_CHUNK * LANES
    lane = np.arange(LANES)
    rep = jnp.asarray(lane[None, :] % S5_GROUP == np.arange(S5_GROUP)[:, None], BF16)
    own = jnp.asarray(lane[:, None] // S5_GROUP == lane[None, :] // S5_GROUP, F32)
    per_tile = lambda j, i: (j, 0, 0)
    fix = lambda j, i: (0, 0)
    return pl.pallas_call(
        functools.partial(_s5_scan_kernel, n_steps=n_steps),
        grid=(nt, b),
        in_specs=[pl.BlockSpec((1, t, LANES), lambda j, i: (i, 0, j)),
                  pl.BlockSpec((1,) + ka.shape[1:], per_tile),
                  pl.BlockSpec((1,) + ps.shape[1:], per_tile),
                  pl.BlockSpec((1,) + qt.shape[1:], per_tile),
                  pl.BlockSpec(rep.shape, fix), pl.BlockSpec(own.shape, fix),
                  pl.BlockSpec((1,) + a1.shape[1:], per_tile),
                  pl.BlockSpec((1,) + a2.shape[1:], per_tile)],
        out_specs=pl.BlockSpec((1, t, LANES), lambda j, i: (i, 0, j)),
        out_shape=jax.ShapeDtypeStruct((b, t, d), F32),
        scratch_shapes=[pltpu.VMEM((wide, wide), BF16),
                        pltpu.VMEM((wide, gl * 2 * S5_STATE), BF16),
                        pltpu.VMEM((wide, gl * 2 * S5_STATE), BF16)],
        compiler_params=_cparams(2),
        name="s5_scan",
    )(u, ka, ps, qt, rep, own, a1, a2)


def _s5_out_kernel(x_ref, y_ref, gpre_ref, d_ref, w_ref, gpost_ref, o_ref):
    x = x_ref[...]
    u = _rms(x, gpre_ref[...])
    z = _gelu(y_ref[...] + u * d_ref[...]).astype(BF16)
    ag = _dot(z, w_ref[...])
    d = x.shape[-1]
    m = ag[:, :d] * jax.nn.sigmoid(ag[:, d:])
    o_ref[...] = x + _rms(m, gpost_ref[...])


def _s5_out_call(x2d, y2d, gpre, dskip, wglu, gpost, tm):
    n, d = x2d.shape
    row = lambda i: (i, 0)
    fix = lambda i: (0, 0)
    return pl.pallas_call(
        _s5_out_kernel,
        grid=(n // tm,),
        in_specs=[pl.BlockSpec((tm, d), row), pl.BlockSpec((tm, d), row),
                  pl.BlockSpec((1, d), fix), pl.BlockSpec((1, d), fix),
                  pl.BlockSpec(wglu.shape, fix), pl.BlockSpec((1, d), fix)],
        out_specs=pl.BlockSpec((tm, d), row),
        out_shape=jax.ShapeDtypeStruct((n, d), F32),
        compiler_params=_cparams(1),
        name="s5_glu_out",
    )(x2d, y2d, gpre.reshape(1, d), dskip.reshape(1, d), wglu, gpost.reshape(1, d))


def _s5_layer(x, gpre, gpost, lam_re, lam_im, log_dt, b_re, b_im, c_re, c_im, d_skip, w_glu):
    b, t, d = x.shape
    n = b * t
    tm = _row_tile(n, 512)
    assert t % (8 * S5_CHUNK) == 0 and d % LANES == 0 and LANES % S5_GROUP == 0
    ka, ps, qt, a1, a2, n_steps = _s5_tables(lam_re, lam_im, log_dt, b_re, b_im, c_re, c_im, t // S5_CHUNK)
    x2d = x.reshape(n, d)
    u = _norm_call(x2d, gpre, F32, tm)
    y = _s5_scan_call(u.reshape(b, t, d), ka, ps, qt, a1, a2, n_steps)
    out = _s5_out_call(x2d, y.reshape(n, d), gpre, d_skip, w_glu.astype(BF16), gpost, tm)
    return out.reshape(b, t, d)


def _ffn_kernel(xprev_ref, x_ref, gpre_ref, wg_ref, wv_ref, cw_ref, cb_ref, wo_ref, gpost_ref,
                o_ref, xn_ref, acc_ref, *, seq_tiles):
    i = pl.program_id(0)
    j = pl.program_id(1)
    h = CONV_HALO

    @pl.when(j == 0)
    def _():
        g = gpre_ref[...]
        xn_ref[h:, :] = _rms(x_ref[...], g).astype(BF16)
        keep = jnp.where(i % seq_tiles == 0, 0.0, 1.0)
        xn_ref[:h, :] = (_rms(xprev_ref[...], g) * keep).astype(BF16)
        acc_ref[...] = jnp.zeros_like(acc_ref)

    xn = xn_ref[...]
    gate = _dot(xn, wg_ref[...])
    val = _dot(xn[h:], wv_ref[...])
    cw = cw_ref[...]
    conv = (cw[0:1] * pltpu.roll(gate, 2, 0) + cw[1:2] * pltpu.roll(gate, 1, 0)
            + cw[2:3] * gate + cb_ref[...])
    act = _gelu(conv[h:]) * val
    acc_ref[...] += _dot(act.astype(BF16), wo_ref[...])

    @pl.when(j == pl.num_programs(1) - 1)
    def _():
        o_ref[...] = x_ref[...] + _rms(acc_ref[...], gpost_ref[...])


def _ffn_layer(x, gpre, gpost, w_in, conv_w, conv_b, w_out):
    b, t, d = x.shape
    f = w_out.shape[0]
    n = b * t
    tm = _row_tile(t, FFN_ROWS)
    tf = max(w for w in range(LANES, min(f, FFN_HIDDEN_CAP) + 1, LANES) if f % w == 0)
    assert f % tf == 0 and tm % CONV_HALO == 0
    nf = f // tf
    h = CONV_HALO
    x2d = x.reshape(n, d)
    cw = jnp.zeros((8, f), F32).at[:CONV_WIDTH].set(conv_w.astype(F32))
    w_in16 = w_in.astype(BF16)
    out = pl.pallas_call(
        functools.partial(_ffn_kernel, seq_tiles=t // tm),
        grid=(n // tm, nf),
        in_specs=[pl.BlockSpec((h, d), lambda i, j: (jnp.maximum(i * (tm // h) - 1, 0), 0)),
                  pl.BlockSpec((tm, d), lambda i, j: (i, 0)),
                  pl.BlockSpec((1, d), lambda i, j: (0, 0)),
                  pl.BlockSpec((d, tf), lambda i, j: (0, j)),
                  pl.BlockSpec((d, tf), lambda i, j: (0, nf + j)),
                  pl.BlockSpec((8, tf), lambda i, j: (0, j)),
                  pl.BlockSpec((1, tf), lambda i, j: (0, j)),
                  pl.BlockSpec((tf, d), lambda i, j: (j, 0)),
                  pl.BlockSpec((1, d), lambda i, j: (0, 0))],
        out_specs=pl.BlockSpec((tm, d), lambda i, j: (i, 0)),
        out_shape=jax.ShapeDtypeStruct((n, d), F32),
        scratch_shapes=[pltpu.VMEM((tm + h, d), BF16), pltpu.VMEM((tm, d), F32)],
        compiler_params=_cparams(2),
        name="conv_ffn",
    )(x2d, x2d, gpre.reshape(1, d), w_in16, w_in16, cw, conv_b.reshape(1, f).astype(F32),
      w_out.astype(BF16), gpost.reshape(1, d))
    return out.reshape(b, t, d)


def _rope_tables(pos, width):
    half = ROPE_DIM // 2
    inv = ROPE_THETA ** (-jnp.arange(half, dtype=F32) / half)
    ang = pos.astype(F32)[:, None] * inv[None, :]
    cos, sin = jnp.cos(ang), jnp.sin(ang)
    rest = HEAD_DIM - ROPE_DIM
    n = pos.shape[0]
    c = jnp.concatenate([cos, cos, jnp.ones((n, rest), F32)], axis=-1)
    s = jnp.concatenate([-sin, sin, jnp.zeros((n, rest), F32)], axis=-1)
    reps = width // HEAD_DIM
    return jnp.tile(c, (1, reps)), jnp.tile(s, (1, reps))


def _swap_perm(width):
    half = ROPE_DIM // 2
    d = np.arange(width)
    dd = d % HEAD_DIM
    return np.where(dd < half, d + half, np.where(dd < ROPE_DIM, d - half, d))


def _kv_kernel(x_ref, g_ref, wkt_ref, w_ref, cost_ref, sint_ref, kt_ref, v_ref, vs_ref, c_ref):
    wk = v_ref.shape[-1]
    sn = _rms(x_ref[...], g_ref[...]).astype(BF16)
    rt = _dot_nt(wkt_ref[...], sn)
    reps = wk // LANES
    cos = jnp.concatenate([cost_ref[...]] * reps, axis=0)
    sin = jnp.concatenate([sint_ref[...]] * reps, axis=0)
    kt_ref[0] = (rt[:wk] * cos + rt[wk:] * sin).astype(kt_ref.dtype)
    r = _dot(sn, w_ref[...])
    v_ref[...] = r[:, :wk].astype(v_ref.dtype)
    vs_ref[...] = r[:, wk:2 * wk].astype(vs_ref.dtype)
    c_ref[...] = r[:, 2 * wk:]


def _kv_call(x2d, g, wkt, w_rest, cos_t, sin_t, b, t, tm):
    n, d = x2d.shape
    wk = KV_HEADS * 2 * HEAD_DIM
    tpt = t // tm
    row = lambda i: (i, 0)
    fix = lambda i: (0, 0)
    tab = lambda i: (0, i % tpt)
    return pl.pallas_call(
        _kv_kernel,
        grid=(n // tm,),
        in_specs=[pl.BlockSpec((tm, d), row), pl.BlockSpec((1, d), fix),
                  pl.BlockSpec(wkt.shape, fix), pl.BlockSpec(w_rest.shape, fix),
                  pl.BlockSpec((LANES, tm), tab), pl.BlockSpec((LANES, tm), tab)],
        out_specs=[pl.BlockSpec((1, wk, tm), lambda i: (i // tpt, 0, i % tpt)),
                   pl.BlockSpec((tm, wk), row), pl.BlockSpec((tm, wk), row), pl.BlockSpec((tm, wk), row)],
        out_shape=[jax.ShapeDtypeStruct((b, wk, t), BF16), jax.ShapeDtypeStruct((n, wk), BF16),
                   jax.ShapeDtypeStruct((n, wk), BF16), jax.ShapeDtypeStruct((n, wk), F32)],
        compiler_params=_cparams(1),
        name="nsa_kv_proj",
    )(x2d, g.reshape(1, d), wkt, w_rest, cos_t, sin_t)


def _cmp_kernel(z_ref, pe_ref, w1_ref, w2kt_ref, w2v_ref, cost_ref, sint_ref, kt_ref, v_ref, *, n_cmp):
    st = CMP_STRIDE
    c = z_ref.shape[1] // st
    hid = w1_ref.shape[-1] // 2
    z = jnp.concatenate([z_ref[0, pl.ds(l, c, stride=st), :] for l in range(st)], axis=1)
    top = _dot((z + pe_ref[0:1, :]).astype(BF16), w1_ref[0])
    bot = _dot((z + pe_ref[1:2, :]).astype(BF16), w1_ref[1])
    hdn = _gelu(top + pltpu.roll(bot, c - 1, 0)).astype(BF16)
    res = [hdn[:, :hid], hdn[:, hid:]]
    kt = (_dot_nt(w2kt_ref[0], res[0]) * cost_ref[...] + _dot_nt(w2kt_ref[1], res[0]) * sint_ref[...])
    col = lax.broadcasted_iota(jnp.int32, kt.shape, 1)
    kt_ref[0, 0] = jnp.where(col < n_cmp, kt, 0.0).astype(kt_ref.dtype)
    v = _dot(res[1], w2v_ref[...])
    row = lax.broadcasted_iota(jnp.int32, v.shape, 0)
    v_ref[0, 0] = jnp.where(row < n_cmp, v, 0.0).astype(v_ref.dtype)


def _cmp_call(ctok, pe, w1, w2kt, w2v, cos_t, sin_t, n_cmp):
    b, t, wid = ctok.shape
    g = wid // LANES
    c = t // CMP_STRIDE
    fix2 = lambda i, j: (0, 0)
    fix3 = lambda i, j: (0, 0, 0)
    return pl.pallas_call(
        functools.partial(_cmp_kernel, n_cmp=n_cmp),
        grid=(b, g),
        in_specs=[pl.BlockSpec((1, t, LANES), lambda i, j: (i, 0, j)),
                  pl.BlockSpec(pe.shape, fix2), pl.BlockSpec(w1.shape, fix3),
                  pl.BlockSpec(w2kt.shape, fix3), pl.BlockSpec(w2v.shape, fix2),
                  pl.BlockSpec(cos_t.shape, fix2), pl.BlockSpec(sin_t.shape, fix2)],
        out_specs=[pl.BlockSpec((1, 1, LANES, c), lambda i, j: (i, j, 0, 0)),
                   pl.BlockSpec((1, 1, c, LANES), lambda i, j: (i, j, 0, 0))],
        out_shape=[jax.ShapeDtypeStruct((b, g, LANES, c), BF16), jax.ShapeDtypeStruct((b, g, c, LANES), BF16)],
        compiler_params=_cparams(2),
        name="nsa_compress",
    )(ctok, pe, w1, w2kt, w2v, cos_t, sin_t)


def _q_kernel(x_ref, g_ref, w_ref, cos_ref, sin_ref, q_ref, gate_ref):
    d = q_ref.shape[-1]
    hn = _rms(x_ref[...], g_ref[...]).astype(BF16)
    r = _dot(hn, w_ref[...])
    reps = d // LANES
    cos = jnp.concatenate([cos_ref[...]] * reps, axis=1)
    sin = jnp.concatenate([sin_ref[...]] * reps, axis=1)
    q_ref[...] = ((r[:, :d] * cos + r[:, d:2 * d] * sin) * (HEAD_DIM ** -0.5 * LOG2E)).astype(q_ref.dtype)
    gate_ref[...] = jax.nn.sigmoid(r[:, 2 * d:])


def _q_call(x2d, g, w_all, cos, sin, t, tm):
    n, d = x2d.shape
    dq = N_HEADS * HEAD_DIM
    tpt = t // tm
    row = lambda i: (i, 0)
    fix = lambda i: (0, 0)
    tab = lambda i: (i % tpt, 0)
    return pl.pallas_call(
        _q_kernel,
        grid=(n // tm,),
        in_specs=[pl.BlockSpec((tm, d), row), pl.BlockSpec((1, d), fix),
                  pl.BlockSpec(w_all.shape, fix),
                  pl.BlockSpec((tm, LANES), tab), pl.BlockSpec((tm, LANES), tab)],
        out_specs=[pl.BlockSpec((tm, dq), row), pl.BlockSpec((tm, LANES), row)],
        out_shape=[jax.ShapeDtypeStruct((n, dq), BF16), jax.ShapeDtypeStruct((n, LANES), F32)],
        compiler_params=_cparams(1),
        name="nsa_q_proj",
    )(x2d, g.reshape(1, d), w_all, cos, sin)


ATTN_UNROLL = 8
ATTN_Q_ROWS = 256


def _attn_kernel(q_ref, gate_ref, kct_ref, vc_ref, kt_ref, v2_ref, v2s_ref, ovt_ref, esel_ref, eg_ref,
                 o_ref, kcbd_ref, vcbd_ref, kbd_ref, vbd_ref, sc_ref, *, n_sb, top_n):
    qb = q_ref.shape[1]
    rep = N_HEADS // KV_HEADS
    hd = HEAD_DIM
    n_c = kct_ref.shape[-1]
    n_tiles = kt_ref.shape[-1] // LANES
    i = pl.program_id(2)
    start = i * qb

    @pl.when(i == 0)
    def _():
        kcbd_ref[...] = jnp.zeros_like(kcbd_ref)
        vcbd_ref[...] = jnp.zeros_like(vcbd_ref)
        kbd_ref[...] = jnp.zeros_like(kbd_ref)
        vbd_ref[...] = jnp.zeros_like(vbd_ref)
        vc = vc_ref[0, 0]
        for r in range(rep):
            h = hd * (r % 2)
            kcbd_ref[hd * r:hd * (r + 1), n_c * r:n_c * (r + 1)] = kct_ref[0, 0, 0:hd, :]
            vcbd_ref[n_c * r:n_c * (r + 1), hd * r:hd * (r + 1)] = vc[:, h:h + hd]

    qblk = q_ref[0]

    s1 = _dot(qblk, kcbd_ref[...])
    n_idx = lax.broadcasted_iota(jnp.int32, (qb, n_c), 1)
    t1 = start + lax.broadcasted_iota(jnp.int32, (qb, n_c), 0)
    m1 = n_idx * CMP_STRIDE + (CMP_BLOCK - 1) <= t1
    psum = jnp.zeros((qb, n_c), F32)
    p_parts = []
    for r in range(rep):
        sm = jnp.where(m1, s1[:, n_c * r:n_c * (r + 1)], NEG)
        mrow = jnp.max(sm, axis=1, keepdims=True)
        e = jnp.exp2(sm - jnp.where(mrow > 0.5 * NEG, mrow, 0.0))
        l = jnp.sum(e, axis=1, keepdims=True)
        p = e * (1.0 / jnp.where(l > 0.0, l, 1.0))
        psum = psum + p
        p_parts.append(p.astype(BF16))
    o_c = _dot(jnp.concatenate(p_parts, axis=1), vcbd_ref[...])

    p_hi = psum.astype(BF16)
    p_lo = (psum - p_hi.astype(F32)).astype(BF16)
    imp_t = _dot_nt(ovt_ref[...], p_hi) + _dot_nt(ovt_ref[...], p_lo)
    blk = lax.broadcasted_iota(jnp.int32, (n_sb, qb), 0)
    t_l = start + lax.broadcasted_iota(jnp.int32, (n_sb, qb), 1)
    cur = jnp.right_shift(t_l, int(math.log2(SEL_BLOCK)))
    forced = (blk == 0) | (blk == cur) | (blk == cur - 1)
    score = jnp.where(forced, SEL_FORCE, jnp.where(blk <= cur, imp_t[:n_sb], SEL_NEG))
    blk_f = blk.astype(F32)
    work = score
    sel_t = jnp.zeros((n_sb, qb), F32)
    for _ in range(top_n):
        best = jnp.max(work, axis=0, keepdims=True)
        first = jnp.min(jnp.where(work == best, blk_f, float(n_sb)), axis=0, keepdims=True)
        hit = blk_f == first
        sel_t = jnp.where(hit, 1.0, sel_t)
        work = jnp.where(hit, -jnp.inf, work)
    sel_t = jnp.where(score > 0.5 * SEL_NEG, sel_t, 0.0)
    sel = sel_t.T.astype(BF16)

    lane = lax.broadcasted_iota(jnp.int32, (qb, LANES), 1)
    t_q = start + lax.broadcasted_iota(jnp.int32, (qb, LANES), 0)

    def scores(u, off, row0):
        ktile = kt_ref[0, row0:row0 + hd, pl.ds(off, LANES)]
        for r in range(rep):
            kbd_ref[u, hd * r:hd * (r + 1), LANES * r:LANES * (r + 1)] = ktile
        return _dot(qblk, kbd_ref[u])

    def weighted_values(u, off, ps, v_even, v_odd):
        ve = v_even[0, pl.ds(off, LANES), :]
        vo = v_odd[0, pl.ds(off, LANES), :]
        for r in range(rep):
            h = hd * (r % 2)
            src = ve if r % 2 == 0 else vo
            vbd_ref[u, LANES * r:LANES * (r + 1), hd * r:hd * (r + 1)] = src[:, h:h + hd]
        return _dot(jnp.concatenate(ps, axis=1), vbd_ref[u])

    def seg(a, r):
        return a[:, LANES * r:LANES * (r + 1)]

    def row_bcast(parts, op):
        return [jnp.broadcast_to(op(a, axis=1, keepdims=True), (qb, LANES)) for a in parts]

    def normalise(acc, ls):
        lrow = row_bcast(ls, jnp.sum)
        den = jnp.concatenate([jnp.where(lane < hd, lrow[2 * c], lrow[2 * c + 1]) for c in range(rep // 2)],
                              axis=1)
        return acc / den

    tpq = qb // LANES
    n_w = WINDOW // LANES + tpq
    w_first = i * tpq - WINDOW // LANES

    def win_mask(w):
        kpos = (w_first + w) * LANES + lane
        if w >= n_w - tpq:
            return kpos <= t_q
        if w >= tpq:
            return kpos >= 0
        return (kpos >= 0) & (t_q - kpos < WINDOW)

    w_off = [pl.multiple_of(jnp.maximum(w_first + w, 0) * LANES, LANES) for w in range(n_w)]
    mx = [jnp.full((qb, LANES), NEG, F32) for _ in range(rep)]
    for w in range(n_w):
        s = scores(w, w_off[w], hd)
        mk = win_mask(w)
        sm = [jnp.where(mk, seg(s, r), NEG) for r in range(rep)]
        sc_ref[w] = jnp.concatenate(sm, axis=1)
        mx = [jnp.maximum(mx[r], sm[r]) for r in range(rep)]
    mrow = row_bcast(mx, jnp.max)
    ls = [jnp.zeros((qb, LANES), F32) for _ in range(rep)]
    acc = jnp.zeros((qb, rep * hd), F32)
    for w in range(n_w):
        ps = [jnp.exp2(sc_ref[w, :, LANES * r:LANES * (r + 1)] - mrow[r]) for r in range(rep)]
        ls = [ls[r] + ps[r] for r in range(rep)]
        acc = acc + weighted_values(w, w_off[w], [p.astype(BF16) for p in ps], v2s_ref, v2_ref)
    o_w = normalise(acc, ls)

    gw = rep * hd
    g_hi = gate_ref[0].astype(BF16)
    g_lo = (gate_ref[0] - g_hi.astype(F32)).astype(BF16)
    gx = _dot(g_hi, eg_ref[0]) + _dot(g_lo, eg_ref[0])
    o_cw = gx[:, :gw] * o_c + gx[:, 2 * gw:] * o_w
    g_s = gx[:, gw:2 * gw]

    n_it = lax.div((i + 1) * tpq + ATTN_UNROLL - 1, ATTN_UNROLL)

    def tile(p, u):
        kt = p * ATTN_UNROLL + u
        return kt, pl.multiple_of(jnp.minimum(kt, n_tiles - 1) * LANES, LANES)

    def pass1(p, mx):
        mx = list(mx)
        for u in range(ATTN_UNROLL):
            kt, off = tile(p, u)
            picked = _dot(sel, esel_ref[0:n_sb, pl.ds(off, LANES)])
            mk = (picked > 0.5) & (kt * LANES + lane <= t_q)
            s = scores(u, off, 0)
            sm = [jnp.where(mk, seg(s, r), NEG) for r in range(rep)]
            sc_ref[kt] = jnp.concatenate(sm, axis=1)
            mx = [jnp.maximum(mx[r], sm[r]) for r in range(rep)]
        return tuple(mx)

    mx = lax.fori_loop(0, n_it, pass1, tuple(jnp.full((qb, LANES), NEG, F32) for _ in range(rep)))
    mrow = row_bcast(mx, jnp.max)

    def pass2(p, carry):
        ls, acc = carry
        ls = list(ls)
        for u in range(ATTN_UNROLL):
            kt, off = tile(p, u)
            ps = []
            for r in range(rep):
                e = jnp.exp2(sc_ref[kt, :, LANES * r:LANES * (r + 1)] - mrow[r])
                ls[r] = ls[r] + e
                ps.append(e.astype(BF16))
            acc = acc + weighted_values(u, off, ps, v2_ref, v2s_ref)
        return tuple(ls), acc

    ls, acc = lax.fori_loop(0, n_it, pass2, (tuple(jnp.zeros((qb, LANES), F32) for _ in range(rep)),
                                             jnp.zeros((qb, rep * hd), F32)))
    o_ref[0] = (o_cw + g_s * normalise(acc, ls)).astype(o_ref.dtype)


def _attn_call(q, gates, kct, vc, kt, v2, v2s, overlap_t, esel, expand, n_sb):
    b, t, dq = q.shape
    g = KV_HEADS
    gw = dq // g
    rep = N_HEADS // KV_HEADS
    c = kct.shape[-1]
    qb = min(ATTN_Q_ROWS, t)
    assert qb % LANES == 0 and t % qb == 0 and rep % 2 == 0 and gw == 2 * LANES
    blk_q = pl.BlockSpec((1, qb, gw), lambda bi, gi, i: (bi, i, gi))
    blk_v = pl.BlockSpec((1, t, LANES), lambda bi, gi, i: (bi, 0, gi))
    fix2 = lambda bi, gi, i: (0, 0)
    n_slots = t // LANES + ATTN_UNROLL
    n_bd = max(ATTN_UNROLL, WINDOW // LANES + qb // LANES)
    return pl.pallas_call(
        functools.partial(_attn_kernel, n_sb=n_sb, top_n=min(SEL_TOPN, n_sb)),
        grid=(b, g, t // qb),
        in_specs=[blk_q,
                  pl.BlockSpec((1, qb, LANES), lambda bi, gi, i: (bi, i, 0)),
                  pl.BlockSpec((1, 1, LANES, c), lambda bi, gi, i: (bi, gi, 0, 0)),
                  pl.BlockSpec((1, 1, c, LANES), lambda bi, gi, i: (bi, gi, 0, 0)),
                  pl.BlockSpec((1, LANES, t), lambda bi, gi, i: (bi, gi, 0)),
                  blk_v, blk_v,
                  pl.BlockSpec(overlap_t.shape, fix2), pl.BlockSpec(esel.shape, fix2),
                  pl.BlockSpec((1,) + expand.shape[1:], lambda bi, gi, i: (gi, 0, 0))],
        out_specs=blk_q,
        out_shape=jax.ShapeDtypeStruct((b, t, dq), BF16),
        scratch_shapes=[pltpu.VMEM((rep * HEAD_DIM, rep * c), BF16),
                        pltpu.VMEM((rep * c, rep * HEAD_DIM), BF16),
                        pltpu.VMEM((n_bd, rep * HEAD_DIM, rep * LANES), BF16),
                        pltpu.VMEM((n_bd, rep * LANES, rep * HEAD_DIM), BF16),
                        pltpu.VMEM((n_slots, qb, rep * LANES), F32)],
        compiler_params=_cparams(3),
        name="nsa_attention",
    )(q, gates, kct, vc, kt, v2, v2s, overlap_t, esel, expand)


def _oproj_kernel(x_ref, o_ref, w_ref, gpost_ref, out_ref):
    out_ref[...] = x_ref[...] + _rms(_dot(o_ref[...], w_ref[...]), gpost_ref[...])


def _oproj_call(x2d, o, w_o, gpost, tm):
    n, d = x2d.shape
    dq = o.shape[-1]
    row = lambda i: (i, 0)
    fix = lambda i: (0, 0)
    return pl.pallas_call(
        _oproj_kernel,
        grid=(n // tm,),
        in_specs=[pl.BlockSpec((tm, d), row), pl.BlockSpec((tm, dq), row),
                  pl.BlockSpec(w_o.shape, fix), pl.BlockSpec((1, d), fix)],
        out_specs=pl.BlockSpec((tm, d), row),
        out_shape=jax.ShapeDtypeStruct((n, d), F32),
        compiler_params=_cparams(1),
        name="nsa_out_proj",
    )(x2d, o, w_o, gpost.reshape(1, d))


def _shared_kv(x, kv_norm_g, w_kv, pe_k, pe_v, k_w1, k_w2, v_w1, v_w2):
    b, t, d = x.shape
    n = b * t
    g, hd = KV_HEADS, HEAD_DIM
    tm = _row_tile(t, 512)
    def cols(ta, tb):
        return np.concatenate([np.concatenate([np.arange(hd) + ta * g * hd + gi * hd,
                                               np.arange(hd) + tb * g * hd + gi * hd]) for gi in range(g)])
    k_cols = cols(2, 4)
    wkt = jnp.concatenate([w_kv[:, k_cols], w_kv[:, k_cols[_swap_perm(k_cols.size)]]], axis=1).T.astype(BF16)
    w_rest = jnp.concatenate([w_kv[:, cols(3, 5)], w_kv[:, cols(5, 3)], w_kv[:, cols(0, 1)]], axis=1).astype(BF16)
    cos, sin = _rope_tables(jnp.arange(t), LANES)
    kt, v2, v2s, ctok = _kv_call(x.reshape(n, d), kv_norm_g, wkt, w_rest, cos.T, sin.T, b, t, tm)
    v2 = v2.reshape(b, t, g * 2 * hd)
    v2s = v2s.reshape(b, t, g * 2 * hd)
    c = t // CMP_STRIDE
    n_cmp = c - CMP_BLOCK // CMP_STRIDE + 1
    assert CMP_BLOCK == 2 * CMP_STRIDE
    st, hid = CMP_STRIDE, k_w1.shape[-1]
    pe = jnp.stack([pe_k.reshape(2, st, hd), pe_v.reshape(2, st, hd)], axis=2).reshape(2, st * 2 * hd).astype(F32)
    zw = jnp.zeros((2, st, hd, hid), k_w1.dtype)
    w1 = jnp.stack([jnp.concatenate([k_w1.reshape(2, st, hd, hid), zw], axis=-1),
                    jnp.concatenate([zw, v_w1.reshape(2, st, hd, hid)], axis=-1)], axis=2)
    w1 = w1.reshape(2, st * 2 * hd, 2 * hid).astype(BF16)
    zpad = jnp.zeros_like(k_w2)
    w2kt = jnp.stack([jnp.concatenate([k_w2, zpad], axis=1).T,
                      jnp.concatenate([k_w2[:, _swap_perm(hd)], zpad], axis=1).T]).astype(BF16)
    w2v = jnp.concatenate([v_w2, v_w2], axis=1).astype(BF16)
    pos_c = jnp.arange(c) * CMP_STRIDE + CMP_BLOCK - 1
    cos_c, sin_c = _rope_tables(pos_c, hd)
    zlane = jnp.zeros((c, LANES - hd), F32)
    cos_c = jnp.concatenate([cos_c, zlane], axis=1).T
    sin_c = jnp.concatenate([sin_c, zlane], axis=1).T
    kct, vc = _cmp_call(ctok.reshape(b, t, g * 2 * hd), pe, w1, w2kt, w2v, cos_c, sin_c, n_cmp)
    return kct, vc, kt, v2, v2s, n_cmp


def _nsa_layer(x, gpre, gpost, w_q, w_o, kct, vc, kt, v2, v2s, n_cmp):
    b, t, d = x.shape
    n = b * t
    hd = HEAD_DIM
    dq = N_HEADS * hd
    tm = _row_tile(t, 512)
    wg = jnp.zeros((d, LANES), w_q.dtype).at[:, :3 * N_HEADS].set(w_q[:, dq:])
    w_all = jnp.concatenate([w_q[:, :dq], w_q[:, :dq][:, _swap_perm(dq)], wg], axis=1).astype(BF16)
    cos, sin = _rope_tables(jnp.arange(t), LANES)
    q, gates = _q_call(x.reshape(n, d), gpre, w_all, cos, sin, t, tm)
    n_sb = t // SEL_BLOCK
    assert n_sb % 8 == 0 and n_sb <= LANES
    c = vc.shape[2]
    ci = np.arange(c)[None, :]
    sj = np.arange(LANES)[:, None]
    overlap_t = ((ci * CMP_STRIDE < (sj + 1) * SEL_BLOCK) & (ci * CMP_STRIDE + CMP_BLOCK > sj * SEL_BLOCK)
                 & (ci < n_cmp) & (sj < n_sb)).astype(np.float32)
    esel = (np.arange(LANES)[:, None] == (np.arange(t)[None, :] // SEL_BLOCK)).astype(np.float32)
    gw = dq // KV_HEADS
    expand = np.zeros((KV_HEADS, LANES, 3 * gw), np.float32)
    for gi in range(KV_HEADS):
        hh = gi * (N_HEADS // KV_HEADS) + np.arange(gw) // hd
        for j in range(3):
            expand[gi, hh * 3 + j, j * gw + np.arange(gw)] = 1.0
    o = _attn_call(q.reshape(b, t, dq), gates.reshape(b, t, LANES), kct, vc, kt, v2, v2s,
                   jnp.asarray(overlap_t, dtype=BF16), jnp.asarray(esel, dtype=BF16),
                   jnp.asarray(expand, dtype=BF16), n_sb)
    out = _oproj_call(x.reshape(n, d), o.reshape(n, dq), w_o.astype(BF16), gpost, tm)
    return out.reshape(b, t, d)


def kernel(x, a_lam_re, a_lam_im, a_log_dt, a_b_re, a_b_im, a_c_re, a_c_im, a_d, a_w_glu, b_w_q, b_w_o, kv_norm_g, w_kv, cmp_pe_k, cmp_pe_v, cmp_k_w1, cmp_k_w2, cmp_v_w1, cmp_v_w2, mix_pre_g, mix_post_g, ffn_pre_g, ffn_post_g, ffn_w_in, ffn_conv_w, ffn_conv_b, ffn_w_out):
    depth = mix_pre_g.shape[0]
    n_a = depth // 2
    kv = None
    for layer in range(depth):
        if layer < n_a:
            i = layer
            x = _s5_layer(x, mix_pre_g[layer], mix_post_g[layer], a_lam_re[i], a_lam_im[i], a_log_dt[i],
                          a_b_re[i], a_b_im[i], a_c_re[i], a_c_im[i], a_d[i], a_w_glu[i])
        else:
            j = layer - n_a
            x = _nsa_layer(x, mix_pre_g[layer], mix_post_g[layer], b_w_q[j], b_w_o[j], *kv)
        x = _ffn_layer(x, ffn_pre_g[layer], ffn_post_g[layer], ffn_w_in[layer], ffn_conv_w[layer],
                       ffn_conv_b[layer], ffn_w_out[layer])
        if layer == n_a - 1:
            kv = _shared_kv(x, kv_norm_g, w_kv, cmp_pe_k, cmp_pe_v, cmp_k_w1, cmp_k_w2, cmp_v_w1, cmp_v_w2)
    return x
```
